```python
import math
import jax, jax.numpy as jnp
from jax import lax
import numpy as np

D_MODEL = 1024
BATCH = 8
SEQ = 4096
DEPTH = 2
DEC_BATCH = 128
DEC_SEQ = 1
PAST_LEN = 16384
PAGE_SIZE = 128

EPS = 1e-6
N_SUB = 3
MACARON_W = 0.5
D_FF = 2816
HG_HEADS = 8
HG_KEY = 128
HG_VAL = 128
HG_WIDTH = HG_HEADS * HG_VAL
HG_CHUNK = 64
SSM_HEADS = 16
SSM_HEAD_DIM = 64
SSM_INNER = SSM_HEADS * SSM_HEAD_DIM
SSM_GROUPS = 4
SSM_STATE = 128
SSM_CONV = 4
SSM_CONV_CH = SSM_INNER + 2 * SSM_GROUPS * SSM_STATE
SSM_CHUNK = 64
AB_SIZES = (HG_HEADS * HG_KEY, HG_HEADS * HG_KEY, HG_WIDTH, HG_WIDTH, SSM_INNER, SSM_CONV_CH, SSM_HEADS)
AB_IN = sum(AB_SIZES)
AB_OUT = HG_WIDTH + SSM_INNER
MLA_HEADS = 16
MLA_Q_RANK = 512
MLA_KV_RANK = 256
MLA_NOPE = 128
MLA_ROPE = 64
MLA_V = 128
MLA_SCALE = (MLA_NOPE + MLA_ROPE) ** -0.5
ROPE_THETA = 10000.0
Q_BLOCK = 128

kernel_name = "hgrn2_ssd_mla_macaron_adaln_step"

F32 = jnp.float32


def rmsnorm(x, g):
    xf = x.astype(F32)
    r = lax.rsqrt(jnp.mean(xf * xf, axis=-1, keepdims=True) + EPS)
    return (xf * r).astype(x.dtype) * g


def modulate(x, g, shift, scale):
    return rmsnorm(x, g) * (1 + scale[:, None]) + shift[:, None]


def gated_residual(x, y, g, gate, w):
    return x + w * gate[:, None] * rmsnorm(y, g)


def swiglu(h, wg, wu, wd):
    return (jax.nn.silu(h @ wg) * (h @ wu)) @ wd


def rope(x, pos):
    half = x.shape[-1] // 2
    inv = ROPE_THETA ** (-jnp.arange(half, dtype=F32) / half)
    ang = pos[:, None] * inv[None]
    cos, sin = jnp.cos(ang)[None, :, None], jnp.sin(ang)[None, :, None]
    x1, x2 = x[..., :half].astype(F32), x[..., half:].astype(F32)
    return jnp.concatenate([x1 * cos - x2 * sin, x2 * cos + x1 * sin], axis=-1).astype(x.dtype)


def to_blocks(a, c, n):
    bsz, L = a.shape[:2]
    a = jnp.pad(a, ((0, 0), (0, n * c - L)) + ((0, 0),) * (a.ndim - 2))
    return a.reshape((bsz, n, c) + a.shape[2:]).swapaxes(0, 1)


def hgrn2_chunked(q, k, v, log_f, s0, chunk):
    bsz, L = q.shape[:2]
    c = min(chunk, L)
    n = -(-L // c)
    causal = jnp.tril(jnp.ones((c, c), bool))[None, :, :, None, None]

    def step(s, blk):
        qc, kc, vc, gc = blk
        b = jnp.cumsum(gc, axis=1)
        dec = jnp.exp(jnp.where(causal, b[:, :, None] - b[:, None, :], -jnp.inf))
        att = jnp.einsum('bthk,bshk,btshk->bhts', qc, kc, dec)
        o = jnp.einsum('bhts,bshv->bthv', att, vc) + jnp.einsum('bthk,bhkv->bthv', qc * jnp.exp(b), s)
        b_last = b[:, -1]
        s = jnp.exp(b_last)[..., None] * s + jnp.einsum('bshk,bshv->bhkv', kc * jnp.exp(b_last[:, None] - b), vc)
        return s, o

    blks = (to_blocks(q, c, n), to_blocks(k, c, n), to_blocks(v, c, n), to_blocks(log_f, c, n))
    s, o = lax.scan(step, s0, blks)
    o = o.swapaxes(0, 1).reshape((bsz, n * c) + o.shape[3:])[:, :L]
    return o, s


def ssd_chunked(x, dt, a_neg, bm, cm, s0, chunk):
    bsz, L, nh, pd = x.shape
    ng = bm.shape[2]
    nj = nh // ng
    c = min(chunk, L)
    n = -(-L // c)
    la = (dt * a_neg).reshape(bsz, L, ng, nj)
    xd = (x * dt[..., None]).reshape(bsz, L, ng, nj, pd)
    causal = jnp.tril(jnp.ones((c, c), bool))[None, :, :, None, None]

    def step(s, blk):
        xc, lc, bc, cc = blk
        b = jnp.cumsum(lc, axis=1)
        seg = jnp.exp(jnp.where(causal, b[:, :, None] - b[:, None, :], -jnp.inf))
        cb = jnp.einsum('btgn,bsgn->btsg', cc, bc)
        y = (jnp.einsum('btsg,btsgj,bsgjp->btgjp', cb, seg, xc)
             + jnp.einsum('btgn,bgjnp->btgjp', cc, s) * jnp.exp(b)[..., None])
        b_last = b[:, -1]
        s = (jnp.exp(b_last)[..., None, None] * s
             + jnp.einsum('bsgn,bsgj,bsgjp->bgjnp', bc, jnp.exp(b_last[:, None] - b), xc))
        return s, y

    blks = (to_blocks(xd, c, n), to_blocks(la, c, n), to_blocks(bm, c, n), to_blocks(cm, c, n))
    s, y = lax.scan(step, s0.reshape((bsz, ng, nj) + s0.shape[2:]), blks)
    y = y.swapaxes(0, 1).reshape(bsz, n * c, nh, pd)[:, :L]
    return y, s.reshape((bsz, nh) + s.shape[3:])


def mixer_ab(h, layer, hg_s0, ssm_s0, conv_prev, P):
    bsz, L = h.shape[:2]
    proj = h @ P['w_in_ab']
    q_a, f_a, i_a, g_a, z_b, xbc, dt_raw = jnp.split(proj, np.cumsum(AB_SIZES)[:-1].tolist(), axis=-1)
    lb = jnp.cumsum(jax.nn.softmax(P['hg_lb_logits'].astype(F32), axis=0), axis=0)[layer]
    fr = f_a.astype(F32)
    log_f = jnp.log(lb + (1 - lb) * jax.nn.sigmoid(fr))
    k_a = (1 - lb) * jax.nn.sigmoid(-fr)
    hs = lambda t, d: t.reshape(bsz, L, HG_HEADS, d)
    o_a, hg_s = hgrn2_chunked(hs(jax.nn.silu(q_a.astype(F32)), HG_KEY), hs(k_a, HG_KEY),
                              hs(i_a.astype(F32), HG_VAL), hs(log_f, HG_KEY), hg_s0.astype(F32), HG_CHUNK)
    o_a = rmsnorm(o_a, P['hg_norm'].astype(F32).reshape(HG_HEADS, HG_VAL)).reshape(bsz, L, HG_WIDTH)
    o_a = o_a.astype(h.dtype) * jax.nn.silu(g_a)
    xbc_pad = jnp.concatenate([conv_prev.astype(xbc.dtype), xbc], axis=1)
    conv = lax.conv_general_dilated(xbc_pad, P['ssm_conv_w'][:, None, :], window_strides=(1,),
                                    padding='VALID', dimension_numbers=('NWC', 'WIO', 'NWC'),
                                    feature_group_count=SSM_CONV_CH)
    xbc_c = jax.nn.silu(conv + P['ssm_conv_b'])
    conv_state = xbc_pad[:, -(SSM_CONV - 1):]
    x_b, b_b, c_b = jnp.split(xbc_c, [SSM_INNER, SSM_INNER + SSM_GROUPS * SSM_STATE], axis=-1)
    dt = jax.nn.softplus(dt_raw.astype(F32) + P['ssm_dt_bias'].astype(F32))
    a_neg = -jnp.exp(P['ssm_a_log'].astype(F32))
    xh = x_b.astype(F32).reshape(bsz, L, SSM_HEADS, SSM_HEAD_DIM)
    grp = lambda t: t.astype(F32).reshape(bsz, L, SSM_GROUPS, SSM_STATE)
    y, ssm_s = ssd_chunked(xh, dt, a_neg, grp(b_b), grp(c_b), ssm_s0.astype(F32), SSM_CHUNK)
    y = (y + P['ssm_d'].astype(F32)[:, None] * xh).reshape(bsz, L, SSM_INNER) * jax.nn.silu(z_b.astype(F32))
    y = rmsnorm(y.reshape(bsz, L, SSM_GROUPS, SSM_INNER // SSM_GROUPS),
                P['ssm_norm'].astype(F32).reshape(SSM_GROUPS, -1)).reshape(bsz, L, SSM_INNER).astype(h.dtype)
    out = jnp.concatenate([o_a, y], axis=-1) @ P['w_out_ab']
    return out, hg_s.astype(hg_s0.dtype), ssm_s.astype(ssm_s0.dtype), conv_state


def mla_attention(q_lat, q_pe, ckv, kpe, past_ckv, past_kpe):
    L = q_lat.shape[1]
    outs = []
    for start in range(0, L, Q_BLOCK):
        stop = min(start + Q_BLOCK, L)
        ql, qp = q_lat[:, start:stop], q_pe[:, start:stop]
        kc, kp = ckv[:, :stop], kpe[:, :stop]
        s = (jnp.einsum('bqhr,bkr->bhqk', ql, kc) + jnp.einsum('bqhd,bkd->bhqk', qp, kp)).astype(F32) * MLA_SCALE
        mask = jnp.arange(stop)[None, :] <= jnp.arange(start, stop)[:, None]
        s = jnp.where(mask, s, -jnp.inf)
        if past_ckv is None:
            p = jax.nn.softmax(s, axis=-1).astype(ckv.dtype)
            o = jnp.einsum('bhqk,bkr->bqhr', p, kc)
        else:
            n_past = past_ckv.shape[1]
            sp = (jnp.einsum('bqhr,bkr->bhqk', ql, past_ckv)
                  + jnp.einsum('bqhd,bkd->bhqk', qp, past_kpe)).astype(F32) * MLA_SCALE
            p = jax.nn.softmax(jnp.concatenate([sp, s], axis=-1), axis=-1).astype(ckv.dtype)
            o = (jnp.einsum('bhqk,bkr->bqhr', p[..., :n_past], past_ckv)
                 + jnp.einsum('bhqk,bkr->bqhr', p[..., n_past:], kc))
        outs.append(o)
    return jnp.concatenate(outs, axis=1)


def mixer_mla(h, pos0, past_ckv, past_kpe, P):
    bsz, L = h.shape[:2]
    pos = (pos0 + jnp.arange(L)).astype(F32)
    cq = rmsnorm(h @ P['mla_w_dq'], P['mla_g_q'])
    q = (cq @ P['mla_w_uq']).reshape(bsz, L, MLA_HEADS, MLA_NOPE + MLA_ROPE)
    q_nope, q_pe = q[..., :MLA_NOPE], rope(q[..., MLA_NOPE:], pos)
    kv = h @ P['mla_w_dkv']
    ckv = rmsnorm(kv[..., :MLA_KV_RANK], P['mla_g_kv'])
    kpe = rope(kv[:, :, None, MLA_KV_RANK:], pos)[:, :, 0]
    q_lat = jnp.einsum('blhd,rhd->blhr', q_nope, P['mla_w_uk'])
    o_lat = mla_attention(q_lat, q_pe, ckv, kpe, past_ckv, past_kpe)
    o = jnp.einsum('blhr,rhv->blhv', o_lat, P['mla_w_uv']).reshape(bsz, L, MLA_HEADS * MLA_V)
    return o @ P['mla_w_o'], ckv, kpe


def trunk(x, c, pos0, hg_s, ssm_s, conv_s, past_ckv, past_kpe, P):
    cs = jax.nn.silu(c)
    for layer in range(DEPTH):
        mod = (cs @ P['ada_w'][layer] + P['ada_b'][layer]).reshape(-1, N_SUB, 3, D_MODEL)
        h = modulate(x, P['norm_pre'][layer, 0], mod[:, 0, 0], mod[:, 0, 1])
        y = swiglu(h, P['ffn_wg'][layer, 0], P['ffn_wu'][layer, 0], P['ffn_wd'][layer, 0])
        x = gated_residual(x, y, P['norm_post'][layer, 0], mod[:, 0, 2], MACARON_W)
        h = modulate(x, P['norm_pre'][layer, 1], mod[:, 1, 0], mod[:, 1, 1])
        if layer % 2 == 0:
            y, hg_s, ssm_s, conv_s = mixer_ab(h, layer, hg_s, ssm_s, conv_s, P)
        else:
            y, ckv, kpe = mixer_mla(h, pos0, past_ckv, past_kpe, P)
        x = gated_residual(x, y, P['norm_post'][layer, 1], mod[:, 1, 2], 1.0)
        h = modulate(x, P['norm_pre'][layer, 2], mod[:, 2, 0], mod[:, 2, 1])
        y = swiglu(h, P['ffn_wg'][layer, 1], P['ffn_wu'][layer, 1], P['ffn_wd'][layer, 1])
        x = gated_residual(x, y, P['norm_post'][layer, 2], mod[:, 2, 2], MACARON_W)
    return x, hg_s, ssm_s, conv_s, ckv, kpe


def setup_inputs(seed: int = 0) -> dict:
    key = jax.random.key(seed)
    ks = iter(jax.random.split(key, 48))
    nrm = lambda shape, s=1.0: s * jax.random.normal(next(ks), shape, F32)
    n_pages = PAST_LEN // PAGE_SIZE
    n_used = DEC_BATCH * n_pages
    n_phys = n_used + max(1, n_used // 4)
    page_table = jax.random.permutation(next(ks), n_phys)[:n_used].reshape(DEC_BATCH, n_pages).astype(jnp.int32)
    dt0 = jnp.exp(jax.random.uniform(next(ks), (SSM_HEADS,), F32, math.log(1e-3), math.log(1e-1)))
    dt_bias = dt0 + jnp.log(-jnp.expm1(-dt0))
    a_log = jnp.log(jax.random.uniform(next(ks), (SSM_HEADS,), F32, 1.0, 16.0))
    d = D_MODEL
    return {
        'x_prompt': nrm((BATCH, SEQ, d)),
        'x_sample': nrm((DEC_BATCH, DEC_SEQ, d)),
        'c_prompt': nrm((BATCH, d)),
        'c_sample': nrm((DEC_BATCH, d)),
        'state_hgrn': nrm((DEC_BATCH, HG_HEADS, HG_KEY, HG_VAL), 0.5),
        'state_ssm': nrm((DEC_BATCH, SSM_HEADS, SSM_STATE, SSM_HEAD_DIM), 0.5),
        'state_conv': nrm((DEC_BATCH, SSM_CONV - 1, SSM_CONV_CH)),
        'cache_ckv': nrm((n_phys, PAGE_SIZE, MLA_KV_RANK)),
        'cache_kpe': nrm((n_phys, PAGE_SIZE, MLA_ROPE)),
        'page_table': page_table,
        'ada_w': nrm((DEPTH, d, N_SUB * 3 * d), 0.5 * d ** -0.5),
        'ada_b': nrm((DEPTH, N_SUB * 3 * d), 0.01),
        'norm_pre': 1.0 + nrm((DEPTH, N_SUB, d), 0.05),
        'norm_post': 1.0 + nrm((DEPTH, N_SUB, d), 0.05),
        'ffn_wg': nrm((DEPTH, 2, d, D_FF), d ** -0.5),
        'ffn_wu': nrm((DEPTH, 2, d, D_FF), d ** -0.5),
        'ffn_wd': nrm((DEPTH, 2, D_FF, d), D_FF ** -0.5),
        'w_in_ab': nrm((d, AB_IN), d ** -0.5),
        'w_out_ab': nrm((AB_OUT, d), AB_OUT ** -0.5),
        'hg_lb_logits': nrm((DEPTH + 1, HG_HEADS * HG_KEY), 0.1),
        'hg_norm': 1.0 + nrm((HG_WIDTH,), 0.05),
        'ssm_conv_w': nrm((SSM_CONV, SSM_CONV_CH), SSM_CONV ** -0.5),
        'ssm_conv_b': nrm((SSM_CONV_CH,), 0.01),
        'ssm_dt_bias': dt_bias,
        'ssm_a_log': a_log,
        'ssm_d': 1.0 + nrm((SSM_HEADS,), 0.1),
        'ssm_norm': 1.0 + nrm((SSM_INNER,), 0.05),
        'mla_w_dq': nrm((d, MLA_Q_RANK), d ** -0.5),
        'mla_g_q': 1.0 + nrm((MLA_Q_RANK,), 0.05),
        'mla_w_uq': nrm((MLA_Q_RANK, MLA_HEADS * (MLA_NOPE + MLA_ROPE)), MLA_Q_RANK ** -0.5),
        'mla_w_dkv': nrm((d, MLA_KV_RANK + MLA_ROPE), d ** -0.5),
        'mla_g_kv': 1.0 + nrm((MLA_KV_RANK,), 0.05),
        'mla_w_uk': nrm((MLA_KV_RANK, MLA_HEADS, MLA_NOPE), MLA_KV_RANK ** -0.5),
        'mla_w_uv': nrm((MLA_KV_RANK, MLA_HEADS, MLA_V), MLA_KV_RANK ** -0.5),
        'mla_w_o': nrm((MLA_HEADS * MLA_V, d), (MLA_HEADS * MLA_V) ** -0.5),
    }


def reference(x_prompt, x_sample, c_prompt, c_sample, state_hgrn, state_ssm, state_conv,
              cache_ckv, cache_kpe, page_table, ada_w, ada_b, norm_pre, norm_post,
              ffn_wg, ffn_wu, ffn_wd, w_in_ab, w_out_ab, hg_lb_logits, hg_norm,
              ssm_conv_w, ssm_conv_b, ssm_dt_bias, ssm_a_log, ssm_d, ssm_norm,
              mla_w_dq, mla_g_q, mla_w_uq, mla_w_dkv, mla_g_kv, mla_w_uk, mla_w_uv, mla_w_o):
    P = dict(ada_w=ada_w, ada_b=ada_b, norm_pre=norm_pre, norm_post=norm_post,
             ffn_wg=ffn_wg, ffn_wu=ffn_wu, ffn_wd=ffn_wd, w_in_ab=w_in_ab, w_out_ab=w_out_ab,
             hg_lb_logits=hg_lb_logits, hg_norm=hg_norm, ssm_conv_w=ssm_conv_w, ssm_conv_b=ssm_conv_b,
             ssm_dt_bias=ssm_dt_bias, ssm_a_log=ssm_a_log, ssm_d=ssm_d, ssm_norm=ssm_norm,
             mla_w_dq=mla_w_dq, mla_g_q=mla_g_q, mla_w_uq=mla_w_uq, mla_w_dkv=mla_w_dkv,
             mla_g_kv=mla_g_kv, mla_w_uk=mla_w_uk, mla_w_uv=mla_w_uv, mla_w_o=mla_w_o)
    bp = x_prompt.shape[0]
    hg0 = jnp.zeros((bp, HG_HEADS, HG_KEY, HG_VAL), state_hgrn.dtype)
    ssm0 = jnp.zeros((bp, SSM_HEADS, SSM_STATE, SSM_HEAD_DIM), state_ssm.dtype)
    conv0 = jnp.zeros((bp, SSM_CONV - 1, SSM_CONV_CH), x_prompt.dtype)
    y_prompt, hg_p, ssm_p, conv_p, ckv_p, kpe_p = trunk(x_prompt, c_prompt, 0, hg0, ssm0, conv0, None, None, P)
    db = x_sample.shape[0]
    past_ckv = cache_ckv[page_table].reshape(db, -1, MLA_KV_RANK)
    past_kpe = cache_kpe[page_table].reshape(db, -1, MLA_ROPE)
    y_sample, hg_s, ssm_s, conv_s, ckv_s, kpe_s = trunk(x_sample, c_sample, past_ckv.shape[1], state_hgrn,
                                                        state_ssm, state_conv, past_ckv, past_kpe, P)
    return (y_prompt, y_sample, hg_p, hg_s, ssm_p, ssm_s, conv_p, conv_s, ckv_p, ckv_s, kpe_p, kpe_s)
```

```python
import functools
import math

import jax
import jax.numpy as jnp
import numpy as np
from jax import lax
from jax.experimental import pallas as pl
from jax.experimental.pallas import tpu as pltpu

F32 = jnp.float32
BF16 = jnp.bfloat16
HI = lax.Precision.HIGHEST

EPS = 1e-6
MACARON_W = 0.5
ROPE_THETA = 10000.0
PAGE_SIZE = 128

LANES = 128
SUBLANES = 8
VMEM_LIMIT = 56 * 1024 * 1024

HG_HEADS = 8
HG_KEY = 128
HG_VAL = 128
SSM_HEADS = 16
SSM_HEAD_DIM = 64
SSM_GROUPS = 4
SSM_STATE = 128
SSM_CONV = 4
SSM_INNER = SSM_HEADS * SSM_HEAD_DIM
SSM_GROUP_W = SSM_INNER // SSM_GROUPS
HEADS_PER_GROUP = SSM_HEADS // SSM_GROUPS
MLA_HEADS = 16
MLA_Q_RANK = 512
MLA_KV_RANK = 256
MLA_NOPE = 128
MLA_ROPE = 64
MLA_V = 128
MLA_SCALE = (MLA_NOPE + MLA_ROPE) ** -0.5
HG_SUB = 16


def _cparams(sem):
    return pltpu.CompilerParams(dimension_semantics=sem, vmem_limit_bytes=VMEM_LIMIT)


def _dot(a, b, cdt):
    if cdt == F32:
        return jnp.dot(a.astype(F32), b.astype(F32), precision=HI, preferred_element_type=F32)
    return jnp.dot(a.astype(BF16), b.astype(BF16), preferred_element_type=F32)


def _dot_nt(a, b, cdt):
    dn = (((1,), (1,)), ((), ()))
    if cdt == F32:
        return lax.dot_general(a.astype(F32), b.astype(F32), dn, precision=HI, preferred_element_type=F32)
    return lax.dot_general(a.astype(BF16), b.astype(BF16), dn, preferred_element_type=F32)


def _dot_hi(a, b):
    return jnp.dot(a, b, precision=HI, preferred_element_type=F32)


def _rms(x, g):
    r = lax.rsqrt(jnp.mean(x * x, axis=-1, keepdims=True) + EPS)
    return (x * r) * g


def _sigmoid(x):
    return 1.0 / (1.0 + jnp.exp(-x))


def _silu(x):
    return x * _sigmoid(x)


def _cumsum_rows(x):
    n = x.shape[0]
    row = lax.broadcasted_iota(jnp.int32, (n, 1), 0)
    s = 1
    while s < n:
        x = x + jnp.where(row >= s, pltpu.roll(x, s, 0), 0.0)
        s *= 2
    return x


def _tile(n, pref):
    t = min(n, pref)
    while n % t:
        t -= SUBLANES
    return t


def _mod_spec(per_row, tm, tiles_per_seq, d):
    if per_row:
        return pl.BlockSpec((1, tm, d), lambda i, *_: (0, i, 0))
    return pl.BlockSpec((1, 1, d), lambda i, *_: (i // tiles_per_seq, 0, 0))


def _ada_kernel(c_ref, w_ref, b_ref, o_ref):
    cs = _silu(c_ref[...])
    o_ref[0] = _dot_hi(cs, w_ref[0]) + b_ref[0]


def _ada_mod(c_all, ada_w, ada_b):
    depth, d, n = ada_w.shape
    m = c_all.shape[0]
    tn = _tile(n, 1152)
    return pl.pallas_call(
        _ada_kernel,
        grid=(depth, n // tn),
        in_specs=[pl.BlockSpec((m, d), lambda l, j: (0, 0)),
                  pl.BlockSpec((1, d, tn), lambda l, j: (l, 0, j)),
                  pl.BlockSpec((1, 1, tn), lambda l, j: (l, 0, j))],
        out_specs=pl.BlockSpec((1, m, tn), lambda l, j: (l, 0, j)),
        out_shape=jax.ShapeDtypeStruct((depth, m, n), F32),
        compiler_params=_cparams(("parallel", "parallel")),
        name="ada_mod",
    )(c_all, ada_w, ada_b.reshape(depth, 1, n))


def _ffn_kernel(x_ref, gpre_ref, sh_ref, sc_ref, gt_ref, gpost_ref, wg_ref, wu_ref, wd_ref,
                o_ref, h_scr, acc_scr, *, cdt, nj):
    j = pl.program_id(1)

    @pl.when(j == 0)
    def _():
        h = _rms(x_ref[...], gpre_ref[...]) * (1.0 + sc_ref[0]) + sh_ref[0]
        h_scr[...] = h.astype(h_scr.dtype)
        acc_scr[...] = jnp.zeros_like(acc_scr)

    h = h_scr[...]
    g = _dot(h, wg_ref[...], cdt)
    u = _dot(h, wu_ref[...], cdt)
    acc_scr[...] += _dot(_silu(g) * u, wd_ref[...], cdt)

    @pl.when(j == nj - 1)
    def _():
        y = _rms(acc_scr[...], gpost_ref[...])
        o_ref[...] = x_ref[...] + MACARON_W * gt_ref[0] * y


def _ffn(x, mods, gpre, gpost, wg, wu, wd, *, per_row, seq_len, cdt, tm_pref, tf):
    n, d = x.shape
    f = wg.shape[1]
    tm = _tile(seq_len if not per_row else n, tm_pref)
    nj = f // tf
    tps = max(seq_len // tm, 1)
    shift, scale, gate = mods
    mspec = _mod_spec(per_row, tm, tps, d)
    vec = pl.BlockSpec((1, d), lambda i, j: (0, 0))
    return pl.pallas_call(
        functools.partial(_ffn_kernel, cdt=cdt, nj=nj),
        grid=(n // tm, nj),
        in_specs=[pl.BlockSpec((tm, d), lambda i, j: (i, 0)), vec, mspec, mspec, mspec, vec,
                  pl.BlockSpec((d, tf), lambda i, j: (0, j)),
                  pl.BlockSpec((d, tf), lambda i, j: (0, j)),
                  pl.BlockSpec((tf, d), lambda i, j: (j, 0))],
        out_specs=pl.BlockSpec((tm, d), lambda i, j: (i, 0)),
        out_shape=jax.ShapeDtypeStruct((n, d), F32),
        scratch_shapes=[pltpu.VMEM((tm, d), cdt), pltpu.VMEM((tm, d), F32)],
        compiler_params=_cparams(("parallel", "arbitrary")),
        name="ffn",
    )(x, gpre, shift, scale, gate, gpost, wg, wu, wd)


def _modmm_kernel(x_ref, gpre_ref, sh_ref, sc_ref, w_ref, wx_ref, o_ref, ox_ref, h_scr, *, cdt):
    j = pl.program_id(1)

    @pl.when(j == 0)
    def _():
        h = _rms(x_ref[...], gpre_ref[...]) * (1.0 + sc_ref[0]) + sh_ref[0]
        h_scr[...] = h.astype(h_scr.dtype)
        ox_ref[...] = _dot(h_scr[...], wx_ref[...], cdt)

    o_ref[...] = _dot(h_scr[...], w_ref[...], cdt)


def _modmm(x, mods, gpre, w, wx, *, per_row, seq_len, cdt, tm_pref, tn):
    n, d = x.shape
    nout = w.shape[1]
    nx = wx.shape[1]
    tm = _tile(seq_len if not per_row else n, tm_pref)
    tps = max(seq_len // tm, 1)
    shift, scale, _ = mods
    mspec = _mod_spec(per_row, tm, tps, d)
    return pl.pallas_call(
        functools.partial(_modmm_kernel, cdt=cdt),
        grid=(n // tm, nout // tn),
        in_specs=[pl.BlockSpec((tm, d), lambda i, j: (i, 0)),
                  pl.BlockSpec((1, d), lambda i, j: (0, 0)), mspec, mspec,
                  pl.BlockSpec((d, tn), lambda i, j: (0, j)),
                  pl.BlockSpec((d, nx), lambda i, j: (0, 0))],
        out_specs=[pl.BlockSpec((tm, tn), lambda i, j: (i, j)),
                   pl.BlockSpec((tm, nx), lambda i, j: (i, 0))],
        out_shape=[jax.ShapeDtypeStruct((n, nout), F32), jax.ShapeDtypeStruct((n, nx), F32)],
        scratch_shapes=[pltpu.VMEM((tm, d), cdt)],
        compiler_params=_cparams(("parallel", "arbitrary")),
        name="modmm",
    )(x, gpre, shift, scale, w, wx)


def _hgrn_kernel(q_ref, f_ref, v_ref, g_ref, lb_ref, gn_ref, s0_ref, o_ref, sout_ref, st_scr,
                 *, chunk, sub, n_inner, l_true, nl, cdt):
    l = pl.program_id(2)
    t_blk = chunk * n_inner

    @pl.when(l == 0)
    def _():
        st_scr[...] = s0_ref[0, 0].T

    lbh = lb_ref[...]
    gn = gn_ref[...]
    rowc = lax.broadcasted_iota(jnp.int32, (chunk, 1), 0)
    nsub = chunk // sub

    for c in range(n_inner):
        r0 = c * chunk
        valid = (l * t_blk + r0 + rowc) < l_true
        qraw = q_ref[r0:r0 + chunk, :]
        fr = f_ref[r0:r0 + chunk, :]
        v = v_ref[r0:r0 + chunk, :]
        f = lbh + (1.0 - lbh) * _sigmoid(fr)
        logf = jnp.where(valid, jnp.log(f), 0.0)
        k = jnp.where(valid, (1.0 - lbh) * _sigmoid(-fr), 0.0)
        q = _silu(qraw)
        b = _cumsum_rows(logf)
        st = st_scr[...]
        o = _dot_nt(q * jnp.exp(b), st, cdt)
        for d in range(sub):
            if d == 0:
                k_sh, b_sh, v_sh = k, b, v
            else:
                k_sh = pltpu.roll(k, d, 0)
                b_sh = pltpu.roll(b, d, 0)
                v_sh = pltpu.roll(v, d, 0)
            same = (rowc & (sub - 1)) >= d
            e = jnp.exp(jnp.where(same, b - b_sh, -jnp.inf))
            a = jnp.sum(q * k_sh * e, axis=-1, keepdims=True)
            o = o + a * v_sh
        if nsub > 1:
            parts = [jnp.zeros((sub, HG_VAL), F32)]
            for i in range(1, nsub):
                lo = i * sub
                ref_b = b[lo - 1:lo, :]
                qi = q[lo:lo + sub, :] * jnp.exp(b[lo:lo + sub, :] - ref_b)
                ki = k[0:lo, :] * jnp.exp(ref_b - b[0:lo, :])
                att = _dot_nt(qi, ki, cdt)
                parts.append(_dot(att, v[0:lo, :], cdt))
            o = o + jnp.concatenate(parts, axis=0)
        bl = b[chunk - 1:chunk, :]
        kt = k * jnp.exp(bl - b)
        st_scr[...] = st * jnp.exp(bl) + _dot(v.T, kt, cdt)
        on = _rms(o, gn)
        o_ref[r0:r0 + chunk, :] = on * _silu(g_ref[r0:r0 + chunk, :])

    @pl.when(l == nl - 1)
    def _():
        sout_ref[0, 0] = st_scr[...].T


def _hgrn(proj, lb, gn, s0, *, bsz, l_pad, l_true, chunk, n_inner, cdt):
    t_blk = chunk * n_inner
    nl = l_pad // t_blk
    n = bsz * l_pad
    sub = min(HG_SUB, chunk)

    def col(cb):
        return pl.BlockSpec((t_blk, HG_KEY), lambda b, h, l: (b * nl + l, cb * HG_HEADS + h))

    hvec = pl.BlockSpec((1, HG_KEY), lambda b, h, l: (0, h))
    sspec = pl.BlockSpec((1, 1, HG_KEY, HG_VAL), lambda b, h, l: (b, h, 0, 0))
    return pl.pallas_call(
        functools.partial(_hgrn_kernel, chunk=chunk, sub=sub, n_inner=n_inner, l_true=l_true, nl=nl, cdt=cdt),
        grid=(bsz, HG_HEADS, nl),
        in_specs=[col(0), col(1), col(2), col(3), hvec, hvec, sspec],
        out_specs=[pl.BlockSpec((t_blk, HG_VAL), lambda b, h, l: (b * nl + l, h)), sspec],
        out_shape=[jax.ShapeDtypeStruct((n, HG_HEADS * HG_VAL), F32),
                   jax.ShapeDtypeStruct(s0.shape, F32)],
        scratch_shapes=[pltpu.VMEM((HG_VAL, HG_KEY), F32)],
        compiler_params=_cparams(("parallel", "parallel", "arbitrary")),
        name="hgrn2",
    )(proj, proj, proj, proj, lb, gn, s0)


def _ssd_kernel(z_ref, xa_ref, xb_ref, dt_ref, cw_ref, cb_ref, dtb_ref, alog_ref, dexp_ref, nrm_ref,
                e_ref, cprev_ref, s0_ref, y_ref, sout_ref, xpad_scr, s_scr, *, chunk, l_true, nl, cdt):
    l = pl.program_id(1)
    c = chunk
    gw = SSM_GROUP_W

    @pl.when(l == 0)
    def _():
        xpad_scr[0:SUBLANES, :] = cprev_ref[0]
        s_scr[...] = s0_ref[0]

    @pl.when(l > 0)
    def _():
        xpad_scr[0:SUBLANES, :] = xpad_scr[c:c + SUBLANES, :]

    xpad_scr[SUBLANES:SUBLANES + c, 0:SSM_INNER] = xa_ref[...]
    xpad_scr[SUBLANES:SUBLANES + c, SSM_INNER:2 * SSM_INNER] = xb_ref[...]
    conv = cb_ref[...]
    for i in range(SSM_CONV):
        lo = SUBLANES - (SSM_CONV - 1) + i
        conv = conv + cw_ref[i:i + 1, :] * xpad_scr[lo:lo + c, :]
    xc = _silu(conv)
    xh = xc[:, 0:SSM_INNER]
    bm = xc[:, SSM_INNER:SSM_INNER + SSM_GROUPS * SSM_STATE]
    cm = xc[:, SSM_INNER + SSM_GROUPS * SSM_STATE:]

    rowc = lax.broadcasted_iota(jnp.int32, (c, 1), 0)
    valid = (l * c + rowc) < l_true
    draw = dt_ref[...] + dtb_ref[...]
    dt = jnp.maximum(draw, 0.0) + jnp.log1p(jnp.exp(-jnp.abs(draw)))
    dt = jnp.where(valid, dt, 0.0)
    la = dt * (-jnp.exp(alog_ref[...]))
    ri = lax.broadcasted_iota(jnp.int32, (c, c), 0)
    ci = lax.broadcasted_iota(jnp.int32, (c, c), 1)
    causal = ri >= ci
    b = _cumsum_rows(la)
    b_t = b.T
    bl = b[c - 1:c, :]
    e = e_ref[...]
    dt_e = _dot_hi(dt, e)
    eb_e = _dot_hi(jnp.exp(b), e)
    w_e = _dot_hi(jnp.exp(bl - b), e)
    xd = xh * dt_e
    lane = lax.broadcasted_iota(jnp.int32, (1, gw), 1)

    ys = []
    for g in range(SSM_GROUPS):
        bg = bm[:, g * SSM_STATE:(g + 1) * SSM_STATE]
        cg = cm[:, g * SSM_STATE:(g + 1) * SSM_STATE]
        gs = slice(g * gw, (g + 1) * gw)
        xdg = xd[:, gs]
        s_g = s_scr[g]
        cb = _dot_nt(cg, bg, cdt)
        yg = _dot(cg, s_g, cdt) * eb_e[:, gs]
        for j in range(HEADS_PER_GROUP):
            h = g * HEADS_PER_GROUP + j
            seg = jnp.exp(jnp.where(causal, b[:, h:h + 1] - b_t[h:h + 1, :], -jnp.inf))
            in_head = (lane >= j * SSM_HEAD_DIM) & (lane < (j + 1) * SSM_HEAD_DIM)
            yg = yg + _dot(cb * seg, jnp.where(in_head, xdg, 0.0), cdt)
        s_scr[g] = eb_e[c - 1:c, gs] * s_g + _dot(bg.T, xdg * w_e[:, gs], cdt)
        ys.append(yg)
    y = jnp.concatenate(ys, axis=-1)
    y = (y + dexp_ref[...] * xh) * _silu(z_ref[...])
    outs = []
    for g in range(SSM_GROUPS):
        gs = slice(g * gw, (g + 1) * gw)
        outs.append(_rms(y[:, gs], nrm_ref[:, gs]))
    y_ref[...] = jnp.concatenate(outs, axis=-1)

    @pl.when(l == nl - 1)
    def _():
        sout_ref[0] = s_scr[...]


def _ssd(proj, dtraw, cw, cb, dtb, alog, dexp, nrm, emat, cprev, s0g, *, bsz, l_pad, l_true, chunk, cdt):
    nl = l_pad // chunk
    n = bsz * l_pad
    w = SSM_INNER
    zcol = 4 * HG_HEADS * HG_KEY // w
    cc = 2 * w

    def col(cbk):
        return pl.BlockSpec((chunk, w), lambda b, l: (b * nl + l, cbk))

    def const(shape):
        return pl.BlockSpec(shape, lambda b, l: (0,) * len(shape))

    sspec = pl.BlockSpec((1, SSM_GROUPS, SSM_STATE, SSM_GROUP_W), lambda b, l: (b, 0, 0, 0))
    return pl.pallas_call(
        functools.partial(_ssd_kernel, chunk=chunk, l_true=l_true, nl=nl, cdt=cdt),
        grid=(bsz, nl),
        in_specs=[col(zcol), col(zcol + 1), col(zcol + 2),
                  pl.BlockSpec((chunk, LANES), lambda b, l: (b * nl + l, 0)),
                  const((SSM_CONV, cc)), const((1, cc)), const((1, LANES)), const((1, LANES)),
                  const((1, w)), const((1, w)), const((LANES, w)),
                  pl.BlockSpec((1, SUBLANES, cc), lambda b, l: (b, 0, 0)), sspec],
        out_specs=[pl.BlockSpec((chunk, w), lambda b, l: (b * nl + l, 0)), sspec],
        out_shape=[jax.ShapeDtypeStruct((n, w), F32), jax.ShapeDtypeStruct(s0g.shape, F32)],
        scratch_shapes=[pltpu.VMEM((chunk + 2 * SUBLANES, cc), F32),
                        pltpu.VMEM((SSM_GROUPS, SSM_STATE, SSM_GROUP_W), F32)],
        compiler_params=_cparams(("parallel", "arbitrary")),
        name="ssd",
    )(proj, proj, proj, dtraw, cw, cb, dtb, alog, dexp, nrm, emat, cprev, s0g)


def _outproj_kernel(x_ref, a_ref, b_ref, w_ref, gt_ref, gpost_ref, o_ref, *, cdt):
    ka = a_ref.shape[1]
    y = _dot(a_ref[...], w_ref[0:ka, :], cdt) + _dot(b_ref[...], w_ref[ka:, :], cdt)
    o_ref[...] = x_ref[...] + gt_ref[0] * _rms(y, gpost_ref[...])


def _outproj(x, a, b, w, gate, gpost, *, per_row, seq_len, cdt, tm_pref):
    n, d = x.shape
    tm = _tile(seq_len if not per_row else n, tm_pref)
    tps = max(seq_len // tm, 1)
    mspec = _mod_spec(per_row, tm, tps, d)
    row = lambda width: pl.BlockSpec((tm, width), lambda i: (i, 0))
    return pl.pallas_call(
        functools.partial(_outproj_kernel, cdt=cdt),
        grid=(n // tm,),
        in_specs=[row(d), row(a.shape[1]), row(b.shape[1]),
                  pl.BlockSpec(w.shape, lambda i: (0, 0)), mspec,
                  pl.BlockSpec((1, d), lambda i: (0, 0))],
        out_specs=row(d),
        out_shape=jax.ShapeDtypeStruct((n, d), F32),
        compiler_params=_cparams(("parallel",)),
        name="outproj",
    )(x, a, b, w, gate, gpost)


def _rope(xs, cs_t):
    u = xs * cs_t
    return (u + pltpu.roll(u, MLA_ROPE, 1))[:, 0:MLA_ROPE]


def _mlaq_kernel(qr_ref, kv_ref, gq_ref, gkv_ref, wuq_ref, wuk_ref, cs_ref,
                 ql_ref, qp_ref, ckv_ref, kpe_ref, ckvb_ref, kpeb_ref, cq_scr, *, cdt):
    h = pl.program_id(1)
    cs_t = cs_ref[...]

    @pl.when(h == 0)
    def _():
        cq_scr[...] = _rms(qr_ref[...], gq_ref[...]).astype(cq_scr.dtype)
        kv = kv_ref[...]
        ckv = _rms(kv[:, 0:MLA_KV_RANK], gkv_ref[...])
        kpe = _rope(kv[:, MLA_KV_RANK:MLA_KV_RANK + 2 * MLA_ROPE], cs_t)
        ckv_ref[...] = ckv
        kpe_ref[...] = kpe
        ckvb_ref[...] = ckv.astype(ckvb_ref.dtype)
        kpeb_ref[...] = kpe.astype(kpeb_ref.dtype)

    q = _dot(cq_scr[...], wuq_ref[0], cdt)
    ql_ref[0] = _dot(q[:, 0:MLA_NOPE], wuk_ref[0], cdt).astype(ql_ref.dtype)
    qp_ref[0] = _rope(q[:, MLA_NOPE:MLA_NOPE + 2 * MLA_ROPE], cs_t).astype(qp_ref.dtype)


def _mlaq(qraw, kvraw, gq, gkv, wuq, wuk, cs_t, *, cdt, tm_pref):
    n = qraw.shape[0]
    tm = _tile(n, tm_pref)
    h = MLA_HEADS
    row = lambda width: pl.BlockSpec((tm, width), lambda i, hh: (i, 0))
    const = lambda shape: pl.BlockSpec(shape, lambda i, hh: (0,) * len(shape))
    head = lambda a, b: pl.BlockSpec((1, a, b), lambda i, hh: (hh, 0, 0))
    hrow = lambda width: pl.BlockSpec((1, tm, width), lambda i, hh: (hh, i, 0))
    return pl.pallas_call(
        functools.partial(_mlaq_kernel, cdt=cdt),
        grid=(n // tm, h),
        in_specs=[row(MLA_Q_RANK), row(kvraw.shape[1]), const((1, MLA_Q_RANK)), const((1, MLA_KV_RANK)),
                  head(MLA_Q_RANK, MLA_NOPE + 2 * MLA_ROPE), head(MLA_NOPE, MLA_KV_RANK),
                  row(2 * MLA_ROPE)],
        out_specs=[hrow(MLA_KV_RANK), hrow(MLA_ROPE), row(MLA_KV_RANK), row(MLA_ROPE),
                   row(MLA_KV_RANK), row(MLA_ROPE)],
        out_shape=[jax.ShapeDtypeStruct((h, n, MLA_KV_RANK), cdt),
                   jax.ShapeDtypeStruct((h, n, MLA_ROPE), cdt),
                   jax.ShapeDtypeStruct((n, MLA_KV_RANK), F32),
                   jax.ShapeDtypeStruct((n, MLA_ROPE), F32),
                   jax.ShapeDtypeStruct((n, MLA_KV_RANK), cdt),
                   jax.ShapeDtypeStruct((n, MLA_ROPE), cdt)],
        scratch_shapes=[pltpu.VMEM((tm, MLA_Q_RANK), cdt)],
        compiler_params=_cparams(("parallel", "arbitrary")),
        name="mla_q",
    )(qraw, kvraw, gq, gkv, wuq, wuk, cs_t)


def _attn_kernel(ql_ref, qp_ref, kc_ref, kp_ref, o_ref, m_scr, l_scr, acc_scr, *, tq, tk):
    i = pl.program_id(1)
    j = pl.program_id(2)
    last = (i * tq + tq - 1) // tk
    rows = MLA_HEADS * tq

    @pl.when(j == 0)
    def _():
        m_scr[...] = jnp.full_like(m_scr, -jnp.inf)
        l_scr[...] = jnp.zeros_like(l_scr)
        acc_scr[...] = jnp.zeros_like(acc_scr)

    def step(masked):
        ql = ql_ref[...].reshape(rows, MLA_KV_RANK)
        qp = qp_ref[...].reshape(rows, MLA_ROPE)
        kc = kc_ref[...]
        s = (_dot_nt(ql, kc, BF16) + _dot_nt(qp, kp_ref[...], BF16)) * MLA_SCALE
        if masked:
            qpos = i * tq + (lax.broadcasted_iota(jnp.int32, (rows, tk), 0) & (tq - 1))
            kpos = j * tk + lax.broadcasted_iota(jnp.int32, (rows, tk), 1)
            s = jnp.where(kpos <= qpos, s, -jnp.inf)
        m_prev = m_scr[...]
        m_new = jnp.maximum(m_prev, jnp.max(s, axis=-1, keepdims=True))
        alpha = jnp.exp(m_prev - m_new)
        p = jnp.exp(s - m_new)
        l_scr[...] = alpha * l_scr[...] + jnp.sum(p, axis=-1, keepdims=True)
        acc_scr[...] = alpha * acc_scr[...] + _dot(p, kc, BF16)
        m_scr[...] = m_new

    @pl.when(j < last)
    def _():
        step(False)

    @pl.when(j == last)
    def _():
        step(True)
        o = acc_scr[...] / l_scr[...]
        o_ref[...] = o.reshape(MLA_HEADS, tq, MLA_KV_RANK).astype(o_ref.dtype)


def _attn(ql, qp, kc, kp, *, bsz, seq_len, tq, tk):
    h, n, r = ql.shape
    nq = seq_len // tq
    nk = seq_len // tk
    rows = h * tq

    def qspec(width):
        return pl.BlockSpec((h, tq, width), lambda b, i, j: (0, b * nq + i, 0))

    def kspec(width):
        return pl.BlockSpec((tk, width), lambda b, i, j: (b * nk + jnp.minimum(j, (i * tq + tq - 1) // tk), 0))

    return pl.pallas_call(
        functools.partial(_attn_kernel, tq=tq, tk=tk),
        grid=(bsz, nq, nk),
        in_specs=[qspec(r), qspec(MLA_ROPE), kspec(r), kspec(MLA_ROPE)],
        out_specs=qspec(r),
        out_shape=jax.ShapeDtypeStruct((h, n, r), BF16),
        scratch_shapes=[pltpu.VMEM((rows, 1), F32), pltpu.VMEM((rows, 1), F32), pltpu.VMEM((rows, r), F32)],
        compiler_params=_cparams(("parallel", "parallel", "arbitrary")),
        name="mla_attn",
    )(ql, qp, kc, kp)


def _decode_kernel(pt_ref, ql_ref, qp_ref, ckn_ref, kpn_ref, cache_c, cache_p, o_ref, cbuf, pbuf, sems,
                   *, n_pages, pg):
    bidx = pl.program_id(0)
    n_chunks = n_pages // pg
    tk = pg * PAGE_SIZE

    def page_copies(page, slot, p):
        return (pltpu.make_async_copy(cache_c.at[page], cbuf.at[slot, p], sems.at[0, slot]),
                pltpu.make_async_copy(cache_p.at[page], pbuf.at[slot, p], sems.at[1, slot]))

    def start(cidx, slot):
        for p in range(pg):
            for cp in page_copies(pt_ref[bidx * n_pages + cidx * pg + p], slot, p):
                cp.start()

    def wait(slot):
        for p in range(pg):
            for cp in page_copies(0, slot, p):
                cp.wait()

    ql = ql_ref[0]
    qp = qp_ref[0]
    start(0, 0)

    def body(cidx, carry):
        m_prev, l_prev, acc = carry
        slot = cidx % 2

        @pl.when(cidx + 1 < n_chunks)
        def _():
            start(cidx + 1, 1 - slot)

        wait(slot)
        kc = cbuf[slot].reshape(tk, MLA_KV_RANK)
        kp = pbuf[slot].reshape(tk, MLA_ROPE)
        s = (_dot_nt(ql, kc, BF16) + _dot_nt(qp, kp, BF16)) * MLA_SCALE
        m_new = jnp.maximum(m_prev, jnp.max(s, axis=-1, keepdims=True))
        alpha = jnp.exp(m_prev - m_new)
        p = jnp.exp(s - m_new)
        l_new = alpha * l_prev + jnp.sum(p, axis=-1, keepdims=True)
        acc = alpha * acc + _dot(p, kc, BF16)
        return m_new, l_new, acc

    init = (jnp.full((MLA_HEADS, 1), -jnp.inf, F32), jnp.zeros((MLA_HEADS, 1), F32),
            jnp.zeros((MLA_HEADS, MLA_KV_RANK), F32))
    m_prev, l_prev, acc = lax.fori_loop(0, n_chunks, body, init)
    ckn = ckn_ref[0]
    s_own = (jnp.sum(ql.astype(F32) * ckn, axis=-1, keepdims=True)
             + jnp.sum(qp.astype(F32) * kpn_ref[0], axis=-1, keepdims=True)) * MLA_SCALE
    m_new = jnp.maximum(m_prev, s_own)
    alpha = jnp.exp(m_prev - m_new)
    p_own = jnp.exp(s_own - m_new)
    l_new = alpha * l_prev + p_own
    o_ref[0] = (alpha * acc + p_own * ckn) / l_new


def _decode(page_table, ql, qp, ckn, kpn, cache_c, cache_p, *, pg):
    db, n_pages = page_table.shape
    h, r, dr = MLA_HEADS, MLA_KV_RANK, MLA_ROPE
    grid_spec = pltpu.PrefetchScalarGridSpec(
        num_scalar_prefetch=1,
        grid=(db,),
        in_specs=[pl.BlockSpec((1, h, r), lambda b, pt: (b, 0, 0)),
                  pl.BlockSpec((1, h, dr), lambda b, pt: (b, 0, 0)),
                  pl.BlockSpec((1, 1, r), lambda b, pt: (b, 0, 0)),
                  pl.BlockSpec((1, 1, dr), lambda b, pt: (b, 0, 0)),
                  pl.BlockSpec(memory_space=pl.ANY),
                  pl.BlockSpec(memory_space=pl.ANY)],
        out_specs=pl.BlockSpec((1, h, r), lambda b, pt: (b, 0, 0)),
        scratch_shapes=[pltpu.VMEM((2, pg, PAGE_SIZE, r), F32),
                        pltpu.VMEM((2, pg, PAGE_SIZE, dr), F32),
                        pltpu.SemaphoreType.DMA((2, 2))],
    )
    return pl.pallas_call(
        functools.partial(_decode_kernel, n_pages=n_pages, pg=pg),
        grid_spec=grid_spec,
        out_shape=jax.ShapeDtypeStruct((db, h, r), F32),
        compiler_params=_cparams(("arbitrary",)),
        name="mla_decode",
    )(page_table.reshape(-1), ql, qp, ckn, kpn, cache_c, cache_p)


def _mlaout_kernel(x_ref, ol_ref, wuv_ref, wo_ref, gt_ref, gpost_ref, o_ref, acc_scr, *, cdt, nh):
    h = pl.program_id(1)

    @pl.when(h == 0)
    def _():
        acc_scr[...] = jnp.zeros_like(acc_scr)

    t = _dot(ol_ref[0], wuv_ref[0], cdt)
    acc_scr[...] += _dot(t, wo_ref[0], cdt)

    @pl.when(h == nh - 1)
    def _():
        o_ref[...] = x_ref[...] + gt_ref[0] * _rms(acc_scr[...], gpost_ref[...])


def _mlaout(x, ol, wuv, wo, gate, gpost, *, per_row, seq_len, cdt, tm_pref):
    n, d = x.shape
    nh = ol.shape[0]
    tm = _tile(seq_len if not per_row else n, tm_pref)
    tps = max(seq_len // tm, 1)
    mspec = _mod_spec(per_row, tm, tps, d)
    return pl.pallas_call(
        functools.partial(_mlaout_kernel, cdt=cdt, nh=nh),
        grid=(n // tm, nh),
        in_specs=[pl.BlockSpec((tm, d), lambda i, h: (i, 0)),
                  pl.BlockSpec((1, tm, MLA_KV_RANK), lambda i, h: (h, i, 0)),
                  pl.BlockSpec((1, MLA_KV_RANK, MLA_V), lambda i, h: (h, 0, 0)),
                  pl.BlockSpec((1, MLA_V, d), lambda i, h: (h, 0, 0)),
                  mspec, pl.BlockSpec((1, d), lambda i, h: (0, 0))],
        out_specs=pl.BlockSpec((tm, d), lambda i, h: (i, 0)),
        out_shape=jax.ShapeDtypeStruct((n, d), F32),
        scratch_shapes=[pltpu.VMEM((tm, d), F32)],
        compiler_params=_cparams(("parallel", "arbitrary")),
        name="mla_out",
    )(x, ol, wuv, wo, gate, gpost)


def _prep_weights(p, cdt):
    d = p['w_in_ab'].shape[0]
    n_main = 4 * HG_HEADS * HG_KEY + SSM_INNER + 2 * SSM_INNER
    w_in = p['w_in_ab']
    w_dt = jnp.pad(w_in[:, n_main:], ((0, 0), (0, LANES - SSM_HEADS)))
    exch = lambda a: jnp.concatenate([a, jnp.roll(a[..., -MLA_ROPE:], MLA_ROPE // 2, axis=-1)], axis=-1)
    w_dkv = exch(p['mla_w_dkv'])
    w_uq = exch(p['mla_w_uq'].reshape(MLA_Q_RANK, MLA_HEADS, MLA_NOPE + MLA_ROPE)).transpose(1, 0, 2)
    c = lambda a: a.astype(cdt)
    return dict(
        ffn_wg=c(p['ffn_wg']), ffn_wu=c(p['ffn_wu']), ffn_wd=c(p['ffn_wd']),
        w_in=c(w_in[:, :n_main]), w_dt=c(w_dt), w_out=c(p['w_out_ab']),
        w_dq=c(p['mla_w_dq']), w_dkv=c(w_dkv),
        w_uq=c(w_uq),
        w_uk=c(p['mla_w_uk'].transpose(1, 2, 0)),
        w_uv=c(p['mla_w_uv'].transpose(1, 0, 2)),
        w_o=c(p['mla_w_o'].reshape(MLA_HEADS, MLA_V, d)),
    )


def _rope_tables(pos):
    half = MLA_ROPE // 2
    inv = ROPE_THETA ** (-jnp.arange(half, dtype=F32) / half)
    ang = pos[:, None] * inv[None]
    cos, sin = jnp.cos(ang), jnp.sin(ang)
    return jnp.concatenate([cos, cos, -sin, sin], axis=-1)


def _trunk(x, mod_all, pos, hg_s0, ssm_s0, conv_prev, past, p, w, consts, *, bsz, seq_len, cdt, cfg):
    n, d = x.shape
    per_row = seq_len == 1
    vec = lambda a: a.reshape(1, -1)

    def mods(layer, sub):
        m = mod_all[layer].reshape(bsz, 3, 3, d)[:, sub]
        if per_row:
            return tuple(m[:, k].reshape(1, bsz, d) for k in range(3))
        return tuple(m[:, k].reshape(bsz, 1, d) for k in range(3))

    common = dict(per_row=per_row, seq_len=seq_len, cdt=cdt)

    def ffn(x, layer, which, sub):
        return _ffn(x, mods(layer, sub), vec(p['norm_pre'][layer, sub]), vec(p['norm_post'][layer, sub]),
                    w['ffn_wg'][layer, which], w['ffn_wu'][layer, which], w['ffn_wd'][layer, which],
                    tm_pref=cfg['tm_ffn'], tf=cfg['tf'], **common)

    x = ffn(x, 0, 0, 0)
    m1 = mods(0, 1)
    proj, dtraw = _modmm(x, m1, vec(p['norm_pre'][0, 1]), w['w_in'], w['w_dt'],
                         tm_pref=cfg['tm_mm'], tn=SSM_INNER, **common)
    l_pad = cfg['l_pad']
    if l_pad != seq_len:
        padrows = lambda a: jnp.pad(a.reshape(bsz, seq_len, -1), ((0, 0), (0, l_pad - seq_len), (0, 0))
                                    ).reshape(bsz * l_pad, -1)
        proj_p, dtraw_p = padrows(proj), padrows(dtraw)
    else:
        proj_p, dtraw_p = proj, dtraw
    lb = jnp.cumsum(jax.nn.softmax(p['hg_lb_logits'].astype(F32), axis=0), axis=0)[0]
    o_a, hg_s = _hgrn(proj_p, vec(lb), vec(p['hg_norm']), hg_s0, bsz=bsz, l_pad=l_pad, l_true=seq_len,
                      chunk=cfg['hg_chunk'], n_inner=cfg['hg_inner'], cdt=cdt)
    to_g = lambda s: s.reshape(bsz, SSM_GROUPS, HEADS_PER_GROUP, SSM_STATE, SSM_HEAD_DIM
                               ).transpose(0, 1, 3, 2, 4).reshape(bsz, SSM_GROUPS, SSM_STATE, SSM_GROUP_W)
    from_g = lambda s: s.reshape(bsz, SSM_GROUPS, SSM_STATE, HEADS_PER_GROUP, SSM_HEAD_DIM
                                 ).transpose(0, 1, 3, 2, 4).reshape(bsz, SSM_HEADS, SSM_STATE, SSM_HEAD_DIM)
    cprev8 = jnp.pad(conv_prev, ((0, 0), (SUBLANES - (SSM_CONV - 1), 0), (0, 0)))
    pad16 = lambda a: jnp.pad(a.astype(F32), (0, LANES - SSM_HEADS)).reshape(1, LANES)
    y_b, ssm_sg = _ssd(proj_p, dtraw_p, p['ssm_conv_w'], vec(p['ssm_conv_b']), pad16(p['ssm_dt_bias']),
                       pad16(p['ssm_a_log']), vec(jnp.repeat(p['ssm_d'].astype(F32), SSM_HEAD_DIM)),
                       vec(p['ssm_norm']), consts['emat'], cprev8, to_g(ssm_s0),
                       bsz=bsz, l_pad=l_pad, l_true=seq_len, chunk=cfg['ssd_chunk'], cdt=cdt)
    ssm_s = from_g(ssm_sg)
    if l_pad != seq_len:
        unpad = lambda a: a.reshape(bsz, l_pad, -1)[:, :seq_len].reshape(bsz * seq_len, -1)
        o_a, y_b = unpad(o_a), unpad(y_b)
    xbc = proj[:, 5 * SSM_INNER:7 * SSM_INNER].reshape(bsz, seq_len, 2 * SSM_INNER)
    conv_s = jnp.concatenate([conv_prev, xbc], axis=1)[:, -(SSM_CONV - 1):]
    x = _outproj(x, o_a, y_b, w['w_out'], m1[2], vec(p['norm_post'][0, 1]), tm_pref=cfg['tm_mm'], **common)
    x = ffn(x, 0, 1, 2)

    x = ffn(x, 1, 0, 0)
    m1 = mods(1, 1)
    qraw, kvraw = _modmm(x, m1, vec(p['norm_pre'][1, 1]), w['w_dq'], w['w_dkv'],
                         tm_pref=cfg['tm_mm'], tn=MLA_Q_RANK, **common)
    ql, qp, ckv, kpe, ckv_c, kpe_c = _mlaq(qraw, kvraw, vec(p['mla_g_q']), vec(p['mla_g_kv']),
                                           w['w_uq'], w['w_uk'], _rope_tables(pos),
                                           cdt=cdt, tm_pref=cfg['tm_mm'])
    if past is None:
        ol = _attn(ql, qp, ckv_c, kpe_c, bsz=bsz, seq_len=seq_len, tq=cfg['tq'], tk=cfg['tk'])
    else:
        cache_c, cache_p, page_table = past
        o_dec = _decode(page_table, ql.transpose(1, 0, 2), qp.transpose(1, 0, 2),
                        ckv.reshape(bsz, 1, -1), kpe.reshape(bsz, 1, -1), cache_c, cache_p, pg=cfg['pg'])
        ol = o_dec.transpose(1, 0, 2)
    x = _mlaout(x, ol, w['w_uv'], w['w_o'], m1[2], vec(p['norm_post'][1, 1]), tm_pref=cfg['tm_mm'], **common)
    x = ffn(x, 1, 1, 2)
    return x, hg_s, ssm_s, conv_s, ckv, kpe


def kernel(x_prompt, x_sample, c_prompt, c_sample, state_hgrn, state_ssm, state_conv, cache_ckv, cache_kpe, page_table, ada_w, ada_b, norm_pre, norm_post, ffn_wg, ffn_wu, ffn_wd, w_in_ab, w_out_ab, hg_lb_logits, hg_norm, ssm_conv_w, ssm_conv_b, ssm_dt_bias, ssm_a_log, ssm_d, ssm_norm, mla_w_dq, mla_g_q, mla_w_uq, mla_w_dkv, mla_g_kv, mla_w_uk, mla_w_uv, mla_w_o):
    p = dict(norm_pre=norm_pre, norm_post=norm_post, ffn_wg=ffn_wg, ffn_wu=ffn_wu, ffn_wd=ffn_wd,
             w_in_ab=w_in_ab, w_out_ab=w_out_ab, hg_lb_logits=hg_lb_logits, hg_norm=hg_norm,
             ssm_conv_w=ssm_conv_w, ssm_conv_b=ssm_conv_b, ssm_dt_bias=ssm_dt_bias, ssm_a_log=ssm_a_log,
             ssm_d=ssm_d, ssm_norm=ssm_norm, mla_w_dq=mla_w_dq, mla_g_q=mla_g_q, mla_w_uq=mla_w_uq,
             mla_w_dkv=mla_w_dkv, mla_g_kv=mla_g_kv, mla_w_uk=mla_w_uk, mla_w_uv=mla_w_uv, mla_w_o=mla_w_o)
    bp, seq, d = x_prompt.shape
    db, dseq, _ = x_sample.shape
    assert dseq == 1
    n_pages = page_table.shape[1]
    past_len = n_pages * PAGE_SIZE
    f = ffn_wg.shape[-1]

    head_of = np.arange(SSM_INNER) // SSM_HEAD_DIM
    emat = jnp.asarray((np.arange(LANES)[:, None] == head_of[None, :]).astype(np.float32))
    consts = dict(emat=emat)

    mod_all = _ada_mod(jnp.concatenate([c_prompt, c_sample], axis=0), ada_w, ada_b)

    tf = f // 2 if (f // 2) % LANES == 0 else f
    w_lo = _prep_weights(p, BF16)
    cfg_p = dict(tm_ffn=512, tf=tf, tm_mm=512, l_pad=seq, hg_chunk=min(64, seq), hg_inner=max(1, min(4, seq // 64)),
                 ssd_chunk=min(256, seq), tq=min(128, seq), tk=min(512, seq))
    hg0 = jnp.zeros((bp, HG_HEADS, HG_KEY, HG_VAL), F32)
    ssm0 = jnp.zeros((bp, SSM_HEADS, SSM_STATE, SSM_HEAD_DIM), F32)
    conv0 = jnp.zeros((bp, SSM_CONV - 1, 2 * SSM_INNER), F32)
    pos_p = jnp.tile(jnp.arange(seq, dtype=F32), bp)
    y_p, hg_p, ssm_p, conv_p, ckv_p, kpe_p = _trunk(
        x_prompt.reshape(bp * seq, d), mod_all[:, :bp], pos_p, hg0, ssm0, conv0, None, p, w_lo, consts,
        bsz=bp, seq_len=seq, cdt=BF16, cfg=cfg_p)

    w_hi = _prep_weights(p, F32)
    pg = 16 if n_pages % 16 == 0 else n_pages
    cfg_s = dict(tm_ffn=128, tf=tf, tm_mm=128, l_pad=SUBLANES, hg_chunk=SUBLANES, hg_inner=1,
                 ssd_chunk=SUBLANES, pg=pg)
    pos_s = jnp.full((db,), past_len, F32)
    y_s, hg_s, ssm_s, conv_s, ckv_s, kpe_s = _trunk(
        x_sample.reshape(db, d), mod_all[:, bp:], pos_s, state_hgrn, state_ssm, state_conv,
        (cache_ckv, cache_kpe, page_table), p, w_hi, consts, bsz=db, seq_len=1, cdt=F32, cfg=cfg_s)

    return (y_p.reshape(bp, seq, d), y_s.reshape(db, 1, d), hg_p, hg_s, ssm_p, ssm_s, conv_p, conv_s,
            ckv_p.reshape(bp, seq, -1), ckv_s.reshape(db, 1, -1), kpe_p.reshape(bp, seq, -1),
            kpe_s.reshape(db, 1, -1))
```

```python
import functools
import math

import jax
import jax.numpy as jnp
import numpy as np
from jax import lax
from jax.experimental import pallas as pl
from jax.experimental.pallas import tpu as pltpu

F32 = jnp.float32
BF16 = jnp.bfloat16
HI = lax.Precision.HIGHEST

EPS = 1e-6
MACARON_W = 0.5
ROPE_THETA = 10000.0
PAGE_SIZE = 128

LANES = 128
SUBLANES = 8
VMEM_LIMIT = 56 * 1024 * 1024

HG_HEADS = 8
HG_KEY = 128
HG_VAL = 128
SSM_HEADS = 16
SSM_HEAD_DIM = 64
SSM_GROUPS = 4
SSM_STATE = 128
SSM_CONV = 4
SSM_INNER = SSM_HEADS * SSM_HEAD_DIM
SSM_GROUP_W = SSM_INNER // SSM_GROUPS
HEADS_PER_GROUP = SSM_HEADS // SSM_GROUPS
MLA_HEADS = 16
MLA_Q_RANK = 512
MLA_KV_RANK = 256
MLA_NOPE = 128
MLA_ROPE = 64
MLA_V = 128
MLA_SCALE = (MLA_NOPE + MLA_ROPE) ** -0.5
FFN_AHEAD = 2
MLA_Q_AHEAD = 2
ATTN_AHEAD = 2
HG_SUB = SUBLANES


def _cparams(sem):
    return pltpu.CompilerParams(dimension_semantics=sem, vmem_limit_bytes=VMEM_LIMIT)


def _dot(a, b, cdt):
    if cdt == F32:
        return jnp.dot(a.astype(F32), b.astype(F32), precision=HI, preferred_element_type=F32)
    return jnp.dot(a.astype(BF16), b.astype(BF16), preferred_element_type=F32)


def _dot_nt(a, b, cdt):
    dn = (((1,), (1,)), ((), ()))
    if cdt == F32:
        return lax.dot_general(a.astype(F32), b.astype(F32), dn, precision=HI, preferred_element_type=F32)
    return lax.dot_general(a.astype(BF16), b.astype(BF16), dn, preferred_element_type=F32)


def _dot_hi(a, b):
    return jnp.dot(a, b, precision=HI, preferred_element_type=F32)


def _rms(x, g):
    r = lax.rsqrt(jnp.mean(x * x, axis=-1, keepdims=True) + EPS)
    return (x * r) * g


def _sigmoid(x):
    return 1.0 / (1.0 + jnp.exp(-x))


def _silu(x):
    return x * _sigmoid(x)


def _cumsum_rows(x):
    n = x.shape[0]
    row = lax.broadcasted_iota(jnp.int32, (n, 1), 0)
    s = 1
    while s < n:
        x = x + jnp.where(row >= s, pltpu.roll(x, s, 0), 0.0)
        s *= 2
    return x


def _tile(n, pref):
    t = min(n, pref)
    while n % t:
        t -= SUBLANES
    return t


def _mod_spec(per_row, tm, tiles_per_seq, d):
    if per_row:
        return pl.BlockSpec((1, tm, d), lambda i, *_: (0, i, 0))
    return pl.BlockSpec((1, 1, d), lambda i, *_: (i // tiles_per_seq, 0, 0))


def _ada_kernel(c_ref, w_ref, b_ref, o_ref):
    cs = _silu(c_ref[...])
    o_ref[0] = _dot_hi(cs, w_ref[0]) + b_ref[0]


def _ada_mod(c_all, ada_w, ada_b):
    depth, d, n = ada_w.shape
    m = c_all.shape[0]
    tn = _tile(n, 1152)
    return pl.pallas_call(
        _ada_kernel,
        grid=(depth, n // tn),
        in_specs=[pl.BlockSpec((m, d), lambda l, j: (0, 0)),
                  pl.BlockSpec((1, d, tn), lambda l, j: (l, 0, j)),
                  pl.BlockSpec((1, 1, tn), lambda l, j: (l, 0, j))],
        out_specs=pl.BlockSpec((1, m, tn), lambda l, j: (l, 0, j)),
        out_shape=jax.ShapeDtypeStruct((depth, m, n), F32),
        compiler_params=_cparams(("parallel", "parallel")),
        name="ada_mod",
    )(c_all, ada_w, ada_b.reshape(depth, 1, n))


def _ffn_kernel(x_ref, gpre_ref, sh_ref, sc_ref, gt_ref, gpost_ref, wg_ref, wu_ref, wd_ref,
                o_ref, a_scr, *, cdt, fb):
    x = x_ref[...]
    h = (_rms(x, gpre_ref[...]) * (1.0 + sc_ref[0]) + sh_ref[0]).astype(cdt)
    nb = wg_ref.shape[1] // fb

    def gate_up(s):
        fs = slice(s * fb, (s + 1) * fb)
        return _dot(h, wg_ref[:, fs], cdt), _dot(h, wu_ref[:, fs], cdt)

    ahead = [gate_up(s) for s in range(min(FFN_AHEAD, nb))]
    for s in range(nb):
        g, u = ahead.pop(0)
        if s + FFN_AHEAD < nb:
            ahead.append(gate_up(s + FFN_AHEAD))
        a_scr[:, s * fb:(s + 1) * fb] = (_silu(g) * u).astype(cdt)
    y = _dot(a_scr[...], wd_ref[...], cdt)
    o_ref[...] = x + MACARON_W * gt_ref[0] * _rms(y, gpost_ref[...])


def _ffn(x, mods, gpre, gpost, wg, wu, wd, *, per_row, seq_len, cdt, tm_pref, fb):
    n, d = x.shape
    f = wg.shape[1]
    tm = _tile(seq_len if not per_row else n, tm_pref)
    tps = max(seq_len // tm, 1)
    shift, scale, gate = mods
    mspec = _mod_spec(per_row, tm, tps, d)
    vec = pl.BlockSpec((1, d), lambda i: (0, 0))
    whole = lambda a: pl.BlockSpec(a.shape, lambda i: (0, 0), pipeline_mode=pl.Buffered(1))
    return pl.pallas_call(
        functools.partial(_ffn_kernel, cdt=cdt, fb=fb),
        grid=(n // tm,),
        in_specs=[pl.BlockSpec((tm, d), lambda i: (i, 0)), vec, mspec, mspec, mspec, vec,
                  whole(wg), whole(wu), whole(wd)],
        out_specs=pl.BlockSpec((tm, d), lambda i: (i, 0)),
        out_shape=jax.ShapeDtypeStruct((n, d), F32),
        scratch_shapes=[pltpu.VMEM((tm, f), cdt)],
        compiler_params=_cparams(("parallel",)),
        name="ffn",
    )(x, gpre, shift, scale, gate, gpost, wg, wu, wd)


def _modmm_kernel(x_ref, gpre_ref, sh_ref, sc_ref, w_ref, wx_ref, o_ref, ox_ref, h_scr, *, cdt):
    j = pl.program_id(1)

    @pl.when(j == 0)
    def _():
        h = _rms(x_ref[...], gpre_ref[...]) * (1.0 + sc_ref[0]) + sh_ref[0]
        h_scr[...] = h.astype(h_scr.dtype)
        ox_ref[...] = _dot(h_scr[...], wx_ref[...], cdt)

    o_ref[...] = _dot(h_scr[...], w_ref[...], cdt)


def _modmm(x, mods, gpre, w, wx, *, per_row, seq_len, cdt, tm_pref, tn):
    n, d = x.shape
    nout = w.shape[1]
    nx = wx.shape[1]
    tm = _tile(seq_len if not per_row else n, tm_pref)
    tps = max(seq_len // tm, 1)
    shift, scale, _ = mods
    mspec = _mod_spec(per_row, tm, tps, d)
    return pl.pallas_call(
        functools.partial(_modmm_kernel, cdt=cdt),
        grid=(n // tm, nout // tn),
        in_specs=[pl.BlockSpec((tm, d), lambda i, j: (i, 0)),
                  pl.BlockSpec((1, d), lambda i, j: (0, 0)), mspec, mspec,
                  pl.BlockSpec((d, tn), lambda i, j: (0, j)),
                  pl.BlockSpec((d, nx), lambda i, j: (0, 0))],
        out_specs=[pl.BlockSpec((tm, tn), lambda i, j: (i, j)),
                   pl.BlockSpec((tm, nx), lambda i, j: (i, 0))],
        out_shape=[jax.ShapeDtypeStruct((n, nout), F32), jax.ShapeDtypeStruct((n, nx), F32)],
        scratch_shapes=[pltpu.VMEM((tm, d), cdt)],
        compiler_params=_cparams(("parallel", "arbitrary")),
        name="modmm",
    )(x, gpre, shift, scale, w, wx)


def _hgrn_kernel(q_ref, f_ref, v_ref, g_ref, lb_ref, gn_ref, s0_ref, o_ref, sout_ref, st_scr,
                 *, chunk, n_inner, hps, l_true, nl, cdt):
    l = pl.program_id(2)
    t_blk = chunk * n_inner
    sub = HG_SUB
    nsub = chunk // sub

    @pl.when(l == 0)
    def _():
        for hh in range(hps):
            st_scr[hh] = s0_ref[0, hh].T

    rowc = lax.broadcasted_iota(jnp.int32, (chunk, 1), 0)
    tril = (lax.broadcasted_iota(jnp.int32, (chunk, chunk), 0)
            >= lax.broadcasted_iota(jnp.int32, (chunk, chunk), 1)).astype(F32)
    srow = lax.broadcasted_iota(jnp.int32, (nsub, sub, 1), 1)
    blocks = lambda a: a.reshape(nsub, sub, a.shape[-1])

    for hh in range(hps):
        cols = slice(hh * HG_KEY, (hh + 1) * HG_KEY)
        lbh = lb_ref[:, cols]
        gn = gn_ref[:, cols]
        pre = []
        for c in range(n_inner):
            rows = slice(c * chunk, (c + 1) * chunk)
            valid = (l * t_blk + c * chunk + rowc) < l_true
            fr = f_ref[rows, cols]
            v = v_ref[rows, cols]
            f = lbh + (1.0 - lbh) * _sigmoid(fr)
            logf = jnp.where(valid, jnp.log(f), 0.0)
            k = jnp.where(valid, (1.0 - lbh) * _sigmoid(-fr), 0.0)
            q = _silu(q_ref[rows, cols])
            b = _dot_hi(tril, logf)
            pre.append((rows, q, k, v, b))
        atts = []
        for rows, q, k, v, b in pre:
            att_c = []
            for i in range(1, nsub):
                lo = i * sub
                hi = min(chunk, -(-lo // 16) * 16)
                ref_b = b[lo - 1:lo, :]
                qi = q[lo:lo + sub, :] * jnp.exp(b[lo:lo + sub, :] - ref_b)
                ki = jnp.where(rowc[0:hi] < lo, k[0:hi, :] * jnp.exp(ref_b - b[0:hi, :]), 0.0)
                att_c.append((_dot_nt(qi, ki, cdt), hi))
            atts.append(att_c)
        intra, incs = [], []
        for (rows, q, k, v, b), att_c in zip(pre, atts):
            q3, k3, b3, v3 = blocks(q), blocks(k), blocks(b), blocks(v)
            o3 = jnp.sum(q3 * k3, axis=-1, keepdims=True) * v3
            for d in range(1, sub):
                k_sh = pltpu.roll(k3, d, 1)
                b_sh = pltpu.roll(b3, d, 1)
                v_sh = pltpu.roll(v3, d, 1)
                a = jnp.sum(q3 * k_sh * jnp.exp(b3 - b_sh), axis=-1, keepdims=True)
                o3 = o3 + jnp.where(srow >= d, a, 0.0) * v_sh
            o = o3.reshape(chunk, HG_VAL)
            if nsub > 1:
                parts = [jnp.zeros((sub, HG_VAL), F32)] + [_dot(att, v[0:hi, :], cdt) for att, hi in att_c]
                o = o + jnp.concatenate(parts, axis=0)
            intra.append(o)
            bl = b[chunk - 1:chunk, :]
            incs.append(_dot(v.T, k * jnp.exp(bl - b), cdt))
        st = st_scr[hh]
        for (rows, q, k, v, b), o, inc in zip(pre, intra, incs):
            o = o + _dot_nt(q * jnp.exp(b), st, cdt)
            st = st * jnp.exp(b[chunk - 1:chunk, :]) + inc
            o_ref[rows, cols] = _rms(o, gn) * _silu(g_ref[rows, cols])
        st_scr[hh] = st

    @pl.when(l == nl - 1)
    def _():
        for hh in range(hps):
            sout_ref[0, hh] = st_scr[hh].T


def _hgrn(proj, lb, gn, s0, *, bsz, l_pad, l_true, chunk, n_inner, hps, cdt):
    t_blk = chunk * n_inner
    nl = l_pad // t_blk
    n = bsz * l_pad
    nhb = HG_HEADS // hps
    wide = hps * HG_KEY

    def col(cb):
        return pl.BlockSpec((t_blk, wide), lambda b, h, l: (b * nl + l, cb * nhb + h))

    hvec = pl.BlockSpec((1, wide), lambda b, h, l: (0, h))
    sspec = pl.BlockSpec((1, hps, HG_KEY, HG_VAL), lambda b, h, l: (b, h, 0, 0))
    return pl.pallas_call(
        functools.partial(_hgrn_kernel, chunk=chunk, n_inner=n_inner, hps=hps, l_true=l_true, nl=nl, cdt=cdt),
        grid=(bsz, nhb, nl),
        in_specs=[col(0), col(1), col(2), col(3), hvec, hvec, sspec],
        out_specs=[pl.BlockSpec((t_blk, wide), lambda b, h, l: (b * nl + l, h)), sspec],
        out_shape=[jax.ShapeDtypeStruct((n, HG_HEADS * HG_VAL), F32),
                   jax.ShapeDtypeStruct(s0.shape, F32)],
        scratch_shapes=[pltpu.VMEM((hps, HG_VAL, HG_KEY), F32)],
        compiler_params=_cparams(("parallel", "parallel", "arbitrary")),
        name="hgrn2",
    )(proj, proj, proj, proj, lb, gn, s0)


def _ssd_kernel(z_ref, xa_ref, xb_ref, dt_ref, cw_ref, cb_ref, dtb_ref, alog_ref, dexp_ref, nrm_ref,
                e_ref, cprev_ref, s0_ref, y_ref, sout_ref, xpad_scr, s_scr, *, chunk, l_true, nl, cdt):
    l = pl.program_id(1)
    c = chunk
    gw = SSM_GROUP_W

    @pl.when(l == 0)
    def _():
        xpad_scr[0:SUBLANES, :] = cprev_ref[0]
        s_scr[...] = s0_ref[0]

    @pl.when(l > 0)
    def _():
        xpad_scr[0:SUBLANES, :] = xpad_scr[c:c + SUBLANES, :]

    xpad_scr[SUBLANES:SUBLANES + c, 0:SSM_INNER] = xa_ref[...]
    xpad_scr[SUBLANES:SUBLANES + c, SSM_INNER:2 * SSM_INNER] = xb_ref[...]
    conv = cb_ref[...]
    for i in range(SSM_CONV):
        lo = SUBLANES - (SSM_CONV - 1) + i
        conv = conv + cw_ref[i:i + 1, :] * xpad_scr[lo:lo + c, :]
    xc = _silu(conv)
    xh = xc[:, 0:SSM_INNER]
    bm = xc[:, SSM_INNER:SSM_INNER + SSM_GROUPS * SSM_STATE]
    cm = xc[:, SSM_INNER + SSM_GROUPS * SSM_STATE:]

    rowc = lax.broadcasted_iota(jnp.int32, (c, 1), 0)
    valid = (l * c + rowc) < l_true
    draw = dt_ref[...] + dtb_ref[...]
    dt = jnp.maximum(draw, 0.0) + jnp.log1p(jnp.exp(-jnp.abs(draw)))
    dt = jnp.where(valid, dt, 0.0)
    la = dt * (-jnp.exp(alog_ref[...]))
    ri = lax.broadcasted_iota(jnp.int32, (c, c), 0)
    ci = lax.broadcasted_iota(jnp.int32, (c, c), 1)
    causal = ri >= ci
    b = _cumsum_rows(la)
    b_t = b.T
    bl = b[c - 1:c, :]
    e = e_ref[...]
    dt_e = _dot_hi(dt, e)
    eb_e = _dot_hi(jnp.exp(b), e)
    w_e = _dot_hi(jnp.exp(bl - b), e)
    xd = xh * dt_e
    lane = lax.broadcasted_iota(jnp.int32, (1, gw), 1)

    ys = []
    for g in range(SSM_GROUPS):
        bg = bm[:, g * SSM_STATE:(g + 1) * SSM_STATE]
        cg = cm[:, g * SSM_STATE:(g + 1) * SSM_STATE]
        gs = slice(g * gw, (g + 1) * gw)
        xdg = xd[:, gs]
        s_g = s_scr[g]
        cb = _dot_nt(cg, bg, cdt)
        yg = _dot(cg, s_g, cdt) * eb_e[:, gs]
        for j in range(HEADS_PER_GROUP):
            h = g * HEADS_PER_GROUP + j
            seg = jnp.exp(jnp.where(causal, b[:, h:h + 1] - b_t[h:h + 1, :], -jnp.inf))
            in_head = (lane >= j * SSM_HEAD_DIM) & (lane < (j + 1) * SSM_HEAD_DIM)
            yg = yg + _dot(cb * seg, jnp.where(in_head, xdg, 0.0), cdt)
        s_scr[g] = eb_e[c - 1:c, gs] * s_g + _dot(bg.T, xdg * w_e[:, gs], cdt)
        ys.append(yg)
    y = jnp.concatenate(ys, axis=-1)
    y = (y + dexp_ref[...] * xh) * _silu(z_ref[...])
    outs = []
    for g in range(SSM_GROUPS):
        gs = slice(g * gw, (g + 1) * gw)
        outs.append(_rms(y[:, gs], nrm_ref[:, gs]))
    y_ref[...] = jnp.concatenate(outs, axis=-1)

    @pl.when(l == nl - 1)
    def _():
        sout_ref[0] = s_scr[...]


def _ssd(proj, dtraw, cw, cb, dtb, alog, dexp, nrm, emat, cprev, s0g, *, bsz, l_pad, l_true, chunk, cdt):
    nl = l_pad // chunk
    n = bsz * l_pad
    w = SSM_INNER
    zcol = 4 * HG_HEADS * HG_KEY // w
    cc = 2 * w

    def col(cbk):
        return pl.BlockSpec((chunk, w), lambda b, l: (b * nl + l, cbk))

    def const(shape):
        return pl.BlockSpec(shape, lambda b, l: (0,) * len(shape))

    sspec = pl.BlockSpec((1, SSM_GROUPS, SSM_STATE, SSM_GROUP_W), lambda b, l: (b, 0, 0, 0))
    return pl.pallas_call(
        functools.partial(_ssd_kernel, chunk=chunk, l_true=l_true, nl=nl, cdt=cdt),
        grid=(bsz, nl),
        in_specs=[col(zcol), col(zcol + 1), col(zcol + 2),
                  pl.BlockSpec((chunk, LANES), lambda b, l: (b * nl + l, 0)),
                  const((SSM_CONV, cc)), const((1, cc)), const((1, LANES)), const((1, LANES)),
                  const((1, w)), const((1, w)), const((LANES, w)),
                  pl.BlockSpec((1, SUBLANES, cc), lambda b, l: (b, 0, 0)), sspec],
        out_specs=[pl.BlockSpec((chunk, w), lambda b, l: (b * nl + l, 0)), sspec],
        out_shape=[jax.ShapeDtypeStruct((n, w), F32), jax.ShapeDtypeStruct(s0g.shape, F32)],
        scratch_shapes=[pltpu.VMEM((chunk + 2 * SUBLANES, cc), F32),
                        pltpu.VMEM((SSM_GROUPS, SSM_STATE, SSM_GROUP_W), F32)],
        compiler_params=_cparams(("parallel", "arbitrary")),
        name="ssd",
    )(proj, proj, proj, dtraw, cw, cb, dtb, alog, dexp, nrm, emat, cprev, s0g)


def _outproj_kernel(x_ref, a_ref, b_ref, w_ref, gt_ref, gpost_ref, o_ref, *, cdt):
    ka = a_ref.shape[1]
    y = _dot(a_ref[...], w_ref[0:ka, :], cdt) + _dot(b_ref[...], w_ref[ka:, :], cdt)
    o_ref[...] = x_ref[...] + gt_ref[0] * _rms(y, gpost_ref[...])


def _outproj(x, a, b, w, gate, gpost, *, per_row, seq_len, cdt, tm_pref):
    n, d = x.shape
    tm = _tile(seq_len if not per_row else n, tm_pref)
    tps = max(seq_len // tm, 1)
    mspec = _mod_spec(per_row, tm, tps, d)
    row = lambda width: pl.BlockSpec((tm, width), lambda i: (i, 0))
    return pl.pallas_call(
        functools.partial(_outproj_kernel, cdt=cdt),
        grid=(n // tm,),
        in_specs=[row(d), row(a.shape[1]), row(b.shape[1]),
                  pl.BlockSpec(w.shape, lambda i: (0, 0)), mspec,
                  pl.BlockSpec((1, d), lambda i: (0, 0))],
        out_specs=row(d),
        out_shape=jax.ShapeDtypeStruct((n, d), F32),
        compiler_params=_cparams(("parallel",)),
        name="outproj",
    )(x, a, b, w, gate, gpost)


def _rope(xs, cs_t):
    u = xs * cs_t
    return u + pltpu.roll(u, MLA_ROPE, 1)


def _mlaq_kernel(qr_ref, kv_ref, gq_ref, gkv_ref, wuq_ref, wuk_ref, cs_ref,
                 qx_ref, ckv_ref, kpe_ref, kx_ref, kxt_ref, *, cdt, q_scale):
    cs_t = cs_ref[...]
    lane = lax.broadcasted_iota(jnp.int32, (1, 2 * MLA_ROPE), 1)
    low = lane < MLA_ROPE
    cq = _rms(qr_ref[...], gq_ref[...]).astype(cdt)
    kv = kv_ref[...]
    ckv = _rms(kv[:, 0:MLA_KV_RANK], gkv_ref[...])
    kpe = _rope(kv[:, MLA_KV_RANK:MLA_KV_RANK + 2 * MLA_ROPE], cs_t)
    ckv_ref[...] = ckv
    kpe_ref[...] = kpe[:, 0:MLA_ROPE]
    kx = jnp.concatenate([ckv, jnp.where(low, kpe, 0.0)], axis=-1)
    kx_ref[...] = kx.astype(kx_ref.dtype)
    kxt_ref[...] = kx.T.astype(kxt_ref.dtype)
    up = lambda h: _dot(cq, wuq_ref[h], cdt)
    ahead = [up(h) for h in range(MLA_Q_AHEAD)]
    for h in range(MLA_HEADS):
        q = ahead.pop(0)
        if h + MLA_Q_AHEAD < MLA_HEADS:
            ahead.append(up(h + MLA_Q_AHEAD))
        ql = _dot(q[:, 0:MLA_NOPE], wuk_ref[h], cdt)
        qp = _rope(q[:, MLA_NOPE:MLA_NOPE + 2 * MLA_ROPE], cs_t)
        qx_ref[h, :, 0:MLA_KV_RANK] = (ql * q_scale).astype(qx_ref.dtype)
        qx_ref[h, :, MLA_KV_RANK:] = jnp.where(low, qp * q_scale, 0.0).astype(qx_ref.dtype)


def _mlaq(qraw, kvraw, gq, gkv, wuq, wuk, cs_t, *, cdt, tm_pref, q_scale):
    n = qraw.shape[0]
    tm = _tile(n, tm_pref)
    h = MLA_HEADS
    kw = MLA_KV_RANK + 2 * MLA_ROPE
    row = lambda width: pl.BlockSpec((tm, width), lambda i: (i, 0))
    const = lambda shape: pl.BlockSpec(shape, lambda i: (0,) * len(shape))
    return pl.pallas_call(
        functools.partial(_mlaq_kernel, cdt=cdt, q_scale=q_scale),
        grid=(n // tm,),
        in_specs=[row(MLA_Q_RANK), row(kvraw.shape[1]), const((1, MLA_Q_RANK)), const((1, MLA_KV_RANK)),
                  const(wuq.shape), const(wuk.shape), row(2 * MLA_ROPE)],
        out_specs=[pl.BlockSpec((h, tm, kw), lambda i: (0, i, 0)), row(MLA_KV_RANK), row(MLA_ROPE), row(kw),
                   pl.BlockSpec((kw, tm), lambda i: (0, i))],
        out_shape=[jax.ShapeDtypeStruct((h, n, kw), cdt),
                   jax.ShapeDtypeStruct((n, MLA_KV_RANK), F32),
                   jax.ShapeDtypeStruct((n, MLA_ROPE), F32),
                   jax.ShapeDtypeStruct((n, kw), BF16),
                   jax.ShapeDtypeStruct((kw, n), BF16)],
        compiler_params=_cparams(("parallel",)),
        name="mla_q",
    )(qraw, kvraw, gq, gkv, wuq, wuk, cs_t)


def _attn_kernel(qx_ref, kx_ref, kxt_ref, o_ref, m_scr, l_scr, acc_scr, *, tq, tk, tkc, hc):
    i = pl.program_id(1)
    j = pl.program_id(2)
    last = (i * tq + tq - 1) // tk

    @pl.when(j == 0)
    def _():
        m_scr[...] = jnp.full_like(m_scr, -jnp.inf)
        l_scr[...] = jnp.zeros_like(l_scr)
        acc_scr[...] = jnp.zeros_like(acc_scr)

    rc = hc * tq
    kw = qx_ref.shape[2]

    def key_block(kh, masked):
        ks = slice(kh * tkc, (kh + 1) * tkc)
        kc = kx_ref[ks, 0:MLA_KV_RANK]
        if masked:
            qpos = i * tq + (lax.broadcasted_iota(jnp.int32, (rc, tkc), 0) & (tq - 1))
            kpos = j * tk + kh * tkc + lax.broadcasted_iota(jnp.int32, (rc, tkc), 1)
            keep = kpos <= qpos
        logits = lambda c: _dot(qx_ref[c * hc:(c + 1) * hc].reshape(rc, kw), kxt_ref[:, ks], BF16)
        nblk = MLA_HEADS // hc
        ahead = [logits(c) for c in range(min(ATTN_AHEAD, nblk))]
        for c in range(nblk):
            rs = slice(c * rc, (c + 1) * rc)
            s = ahead.pop(0)
            if c + ATTN_AHEAD < nblk:
                ahead.append(logits(c + ATTN_AHEAD))
            if masked:
                s = jnp.where(keep, s, -jnp.inf)
            m_prev = m_scr[rs]
            m_new = jnp.maximum(m_prev, jnp.max(s, axis=-1, keepdims=True))
            alpha = jnp.exp2(m_prev - m_new)
            p = jnp.exp2(s - m_new)
            l_scr[rs] = alpha * l_scr[rs] + jnp.sum(p, axis=-1, keepdims=True)
            acc_scr[rs] = alpha * acc_scr[rs] + _dot(p, kc, BF16)
            m_scr[rs] = m_new

    @pl.when(j < last)
    def _():
        for kh in range(tk // tkc):
            key_block(kh, False)

    @pl.when(j == last)
    def _():
        key_block(0, True)
        for kh in range(1, tk // tkc):
            @pl.when(i * tq + tq - 1 >= j * tk + kh * tkc)
            def _():
                key_block(kh, True)
        o = acc_scr[...] / l_scr[...]
        o_ref[...] = o.reshape(MLA_HEADS, tq, MLA_KV_RANK).astype(o_ref.dtype)


def _attn(qx, kx, kxt, *, bsz, seq_len, tq, tk, tkc, hc):
    h, n, kw = qx.shape
    r = MLA_KV_RANK
    nq = seq_len // tq
    nk = seq_len // tk
    rows = h * tq
    kblk = lambda b, i, j: b * nk + jnp.minimum(j, (i * tq + tq - 1) // tk)
    return pl.pallas_call(
        functools.partial(_attn_kernel, tq=tq, tk=tk, tkc=tkc, hc=hc),
        grid=(bsz, nq, nk),
        in_specs=[pl.BlockSpec((h, tq, kw), lambda b, i, j: (0, b * nq + i, 0)),
                  pl.BlockSpec((tk, kw), lambda b, i, j: (kblk(b, i, j), 0)),
                  pl.BlockSpec((kw, tk), lambda b, i, j: (0, kblk(b, i, j)))],
        out_specs=pl.BlockSpec((h, tq, r), lambda b, i, j: (0, b * nq + i, 0)),
        out_shape=jax.ShapeDtypeStruct((h, n, r), BF16),
        scratch_shapes=[pltpu.VMEM((rows, 1), F32), pltpu.VMEM((rows, 1), F32), pltpu.VMEM((rows, r), F32)],
        compiler_params=_cparams(("parallel", "parallel", "arbitrary")),
        name="mla_attn",
    )(qx, kx, kxt)


def _decode_kernel(pt_ref, qx_ref, ckn_ref, kpn_ref, cache_c, cache_p, o_ref, cbuf, pbuf, sems,
                   *, n_pages, pg):
    bidx = pl.program_id(0)
    n_chunks = n_pages // pg
    tk = pg * PAGE_SIZE

    def page_copies(page, slot, p):
        return (pltpu.make_async_copy(cache_c.at[page], cbuf.at[slot, p], sems.at[0, slot]),
                pltpu.make_async_copy(cache_p.at[page], pbuf.at[slot, p], sems.at[1, slot]))

    def start(cidx, slot):
        for p in range(pg):
            for cp in page_copies(pt_ref[bidx * n_pages + cidx * pg + p], slot, p):
                cp.start()

    def wait(slot):
        for p in range(pg):
            for cp in page_copies(0, slot, p):
                cp.wait()

    qx = qx_ref[0]
    ql = qx[:, 0:MLA_KV_RANK]
    qp = qx[:, MLA_KV_RANK:MLA_KV_RANK + MLA_ROPE]
    start(0, 0)

    def body(cidx, carry):
        m_prev, l_prev, acc = carry
        slot = cidx % 2

        @pl.when(cidx + 1 < n_chunks)
        def _():
            start(cidx + 1, 1 - slot)

        wait(slot)
        kc = cbuf[slot].reshape(tk, MLA_KV_RANK)
        s_rope = jnp.concatenate([_dot(qp, pbuf[slot, p], BF16) for p in range(pg)], axis=-1)
        s = (_dot_nt(ql, kc, BF16) + s_rope) * MLA_SCALE
        m_new = jnp.maximum(m_prev, jnp.max(s, axis=-1, keepdims=True))
        alpha = jnp.exp(m_prev - m_new)
        p = jnp.exp(s - m_new)
        l_new = alpha * l_prev + jnp.sum(p, axis=-1, keepdims=True)
        acc = alpha * acc + _dot(p, kc, BF16)
        return m_new, l_new, acc

    init = (jnp.full((MLA_HEADS, 1), -jnp.inf, F32), jnp.zeros((MLA_HEADS, 1), F32),
            jnp.zeros((MLA_HEADS, MLA_KV_RANK), F32))
    m_prev, l_prev, acc = lax.fori_loop(0, n_chunks, body, init)
    ckn = ckn_ref[0]
    s_own = (jnp.sum(ql * ckn, axis=-1, keepdims=True)
             + jnp.sum(qp * kpn_ref[0], axis=-1, keepdims=True)) * MLA_SCALE
    m_new = jnp.maximum(m_prev, s_own)
    alpha = jnp.exp(m_prev - m_new)
    p_own = jnp.exp(s_own - m_new)
    l_new = alpha * l_prev + p_own
    o_ref[0] = (alpha * acc + p_own * ckn) / l_new


def _decode(page_table, qx, ckn, kpn, cache_c, cache_pt, *, pg):
    db, n_pages = page_table.shape
    h, r, dr = MLA_HEADS, MLA_KV_RANK, MLA_ROPE
    grid_spec = pltpu.PrefetchScalarGridSpec(
        num_scalar_prefetch=1,
        grid=(db,),
        in_specs=[pl.BlockSpec((1, h, qx.shape[2]), lambda b, pt: (b, 0, 0)),
                  pl.BlockSpec((1, 1, r), lambda b, pt: (b, 0, 0)),
                  pl.BlockSpec((1, 1, dr), lambda b, pt: (b, 0, 0)),
                  pl.BlockSpec(memory_space=pl.ANY),
                  pl.BlockSpec(memory_space=pl.ANY)],
        out_specs=pl.BlockSpec((1, h, r), lambda b, pt: (b, 0, 0)),
        scratch_shapes=[pltpu.VMEM((2, pg, PAGE_SIZE, r), F32),
                        pltpu.VMEM((2, pg, dr, PAGE_SIZE), F32),
                        pltpu.SemaphoreType.DMA((2, 2))],
    )
    return pl.pallas_call(
        functools.partial(_decode_kernel, n_pages=n_pages, pg=pg),
        grid_spec=grid_spec,
        out_shape=jax.ShapeDtypeStruct((db, h, r), F32),
        compiler_params=_cparams(("arbitrary",)),
        name="mla_decode",
    )(page_table.reshape(-1), qx, ckn, kpn, cache_c, cache_pt)


def _mlaout_kernel(x_ref, ol_ref, wuv_ref, wo_ref, gt_ref, gpost_ref, o_ref, *, cdt):
    t = jnp.concatenate([_dot(ol_ref[h], wuv_ref[h], cdt).astype(cdt) for h in range(MLA_HEADS)], axis=-1)
    y = _dot(t, wo_ref[...], cdt)
    o_ref[...] = x_ref[...] + gt_ref[0] * _rms(y, gpost_ref[...])


def _mlaout(x, ol, wuv, wo, gate, gpost, *, per_row, seq_len, cdt, tm_pref):
    n, d = x.shape
    tm = _tile(seq_len if not per_row else n, tm_pref)
    tps = max(seq_len // tm, 1)
    mspec = _mod_spec(per_row, tm, tps, d)
    return pl.pallas_call(
        functools.partial(_mlaout_kernel, cdt=cdt),
        grid=(n // tm,),
        in_specs=[pl.BlockSpec((tm, d), lambda i: (i, 0)),
                  pl.BlockSpec((MLA_HEADS, tm, MLA_KV_RANK), lambda i: (0, i, 0)),
                  pl.BlockSpec(wuv.shape, lambda i: (0, 0, 0)),
                  pl.BlockSpec(wo.shape, lambda i: (0, 0)),
                  mspec, pl.BlockSpec((1, d), lambda i: (0, 0))],
        out_specs=pl.BlockSpec((tm, d), lambda i: (i, 0)),
        out_shape=jax.ShapeDtypeStruct((n, d), F32),
        compiler_params=_cparams(("parallel",)),
        name="mla_out",
    )(x, ol, wuv, wo, gate, gpost)


def _prep_weights(p, cdt):
    d = p['w_in_ab'].shape[0]
    n_main = 4 * HG_HEADS * HG_KEY + SSM_INNER + 2 * SSM_INNER
    w_in = p['w_in_ab']
    w_dt = jnp.pad(w_in[:, n_main:], ((0, 0), (0, LANES - SSM_HEADS)))
    exch = lambda a: jnp.concatenate([a, jnp.roll(a[..., -MLA_ROPE:], MLA_ROPE // 2, axis=-1)], axis=-1)
    w_dkv = exch(p['mla_w_dkv'])
    w_uq = exch(p['mla_w_uq'].reshape(MLA_Q_RANK, MLA_HEADS, MLA_NOPE + MLA_ROPE)).transpose(1, 0, 2)
    c = lambda a: a.astype(cdt)
    return dict(
        ffn_wg=c(p['ffn_wg']), ffn_wu=c(p['ffn_wu']), ffn_wd=c(p['ffn_wd']),
        w_in=c(w_in[:, :n_main]), w_dt=c(w_dt), w_out=c(p['w_out_ab']),
        w_dq=c(p['mla_w_dq']), w_dkv=c(w_dkv),
        w_uq=c(w_uq),
        w_uk=c(p['mla_w_uk'].transpose(1, 2, 0)),
        w_uv=c(p['mla_w_uv'].transpose(1, 0, 2)),
        w_o=c(p['mla_w_o']),
    )


def _rope_tables(pos):
    half = MLA_ROPE // 2
    inv = ROPE_THETA ** (-jnp.arange(half, dtype=F32) / half)
    ang = pos[:, None] * inv[None]
    cos, sin = jnp.cos(ang), jnp.sin(ang)
    return jnp.concatenate([cos, cos, -sin, sin], axis=-1)


def _trunk(x, mod_all, pos, hg_s0, ssm_s0, conv_prev, past, p, w, consts, *, bsz, seq_len, cdt, cfg):
    n, d = x.shape
    per_row = seq_len == 1
    vec = lambda a: a.reshape(1, -1)

    def mods(layer, sub):
        m = mod_all[layer].reshape(bsz, 3, 3, d)[:, sub]
        if per_row:
            return tuple(m[:, k].reshape(1, bsz, d) for k in range(3))
        return tuple(m[:, k].reshape(bsz, 1, d) for k in range(3))

    common = dict(per_row=per_row, seq_len=seq_len, cdt=cdt)

    def ffn(x, layer, which, sub):
        return _ffn(x, mods(layer, sub), vec(p['norm_pre'][layer, sub]), vec(p['norm_post'][layer, sub]),
                    w['ffn_wg'][layer, which], w['ffn_wu'][layer, which], w['ffn_wd'][layer, which],
                    tm_pref=cfg['tm_ffn'], fb=cfg['fb'], **common)

    x = ffn(x, 0, 0, 0)
    m1 = mods(0, 1)
    proj, dtraw = _modmm(x, m1, vec(p['norm_pre'][0, 1]), w['w_in'], w['w_dt'],
                         tm_pref=cfg['tm_mm'], tn=SSM_INNER, **common)
    l_pad = cfg['l_pad']
    if l_pad != seq_len:
        padrows = lambda a: jnp.pad(a.reshape(bsz, seq_len, -1), ((0, 0), (0, l_pad - seq_len), (0, 0))
                                    ).reshape(bsz * l_pad, -1)
        proj_p, dtraw_p = padrows(proj), padrows(dtraw)
    else:
        proj_p, dtraw_p = proj, dtraw
    lb = jnp.cumsum(jax.nn.softmax(p['hg_lb_logits'].astype(F32), axis=0), axis=0)[0]
    o_a, hg_s = _hgrn(proj_p, vec(lb), vec(p['hg_norm']), hg_s0, bsz=bsz, l_pad=l_pad, l_true=seq_len,
                      chunk=cfg['hg_chunk'], n_inner=cfg['hg_inner'], hps=cfg['hg_hps'], cdt=cdt)
    to_g = lambda s: s.reshape(bsz, SSM_GROUPS, HEADS_PER_GROUP, SSM_STATE, SSM_HEAD_DIM
                               ).transpose(0, 1, 3, 2, 4).reshape(bsz, SSM_GROUPS, SSM_STATE, SSM_GROUP_W)
    from_g = lambda s: s.reshape(bsz, SSM_GROUPS, SSM_STATE, HEADS_PER_GROUP, SSM_HEAD_DIM
                                 ).transpose(0, 1, 3, 2, 4).reshape(bsz, SSM_HEADS, SSM_STATE, SSM_HEAD_DIM)
    cprev8 = jnp.pad(conv_prev, ((0, 0), (SUBLANES - (SSM_CONV - 1), 0), (0, 0)))
    pad16 = lambda a: jnp.pad(a.astype(F32), (0, LANES - SSM_HEADS)).reshape(1, LANES)
    y_b, ssm_sg = _ssd(proj_p, dtraw_p, p['ssm_conv_w'], vec(p['ssm_conv_b']), pad16(p['ssm_dt_bias']),
                       pad16(p['ssm_a_log']), vec(jnp.repeat(p['ssm_d'].astype(F32), SSM_HEAD_DIM)),
                       vec(p['ssm_norm']), consts['emat'], cprev8, to_g(ssm_s0),
                       bsz=bsz, l_pad=l_pad, l_true=seq_len, chunk=cfg['ssd_chunk'], cdt=cdt)
    ssm_s = from_g(ssm_sg)
    if l_pad != seq_len:
        unpad = lambda a: a.reshape(bsz, l_pad, -1)[:, :seq_len].reshape(bsz * seq_len, -1)
        o_a, y_b = unpad(o_a), unpad(y_b)
    xbc = proj[:, 5 * SSM_INNER:7 * SSM_INNER].reshape(bsz, seq_len, 2 * SSM_INNER)
    conv_s = jnp.concatenate([conv_prev, xbc], axis=1)[:, -(SSM_CONV - 1):]
    x = _outproj(x, o_a, y_b, w['w_out'], m1[2], vec(p['norm_post'][0, 1]), tm_pref=cfg['tm_mm'], **common)
    x = ffn(x, 0, 1, 2)

    x = ffn(x, 1, 0, 0)
    m1 = mods(1, 1)
    qraw, kvraw = _modmm(x, m1, vec(p['norm_pre'][1, 1]), w['w_dq'], w['w_dkv'],
                         tm_pref=cfg['tm_mm'], tn=MLA_Q_RANK, **common)
    q_scale = 1.0 if past is not None else MLA_SCALE * math.log2(math.e)
    qx, ckv, kpe, kx, kxt = _mlaq(qraw, kvraw, vec(p['mla_g_q']), vec(p['mla_g_kv']), w['w_uq'], w['w_uk'],
                                  _rope_tables(pos), cdt=cdt, tm_pref=cfg['tm_mm'], q_scale=q_scale)
    if past is None:
        ol = _attn(qx, kx, kxt, bsz=bsz, seq_len=seq_len, tq=cfg['tq'], tk=cfg['tk'], tkc=cfg['tkc'], hc=cfg['hc'])
    else:
        cache_c, cache_p, page_table = past
        o_dec = _decode(page_table, qx.transpose(1, 0, 2), ckv.reshape(bsz, 1, -1), kpe.reshape(bsz, 1, -1),
                        cache_c, cache_p.transpose(0, 2, 1), pg=cfg['pg'])
        ol = o_dec.transpose(1, 0, 2)
    x = _mlaout(x, ol, w['w_uv'], w['w_o'], m1[2], vec(p['norm_post'][1, 1]), tm_pref=cfg['tm_mm'], **common)
    x = ffn(x, 1, 1, 2)
    return x, hg_s, ssm_s, conv_s, ckv, kpe


def kernel(x_prompt, x_sample, c_prompt, c_sample, state_hgrn, state_ssm, state_conv, cache_ckv, cache_kpe, page_table, ada_w, ada_b, norm_pre, norm_post, ffn_wg, ffn_wu, ffn_wd, w_in_ab, w_out_ab, hg_lb_logits, hg_norm, ssm_conv_w, ssm_conv_b, ssm_dt_bias, ssm_a_log, ssm_d, ssm_norm, mla_w_dq, mla_g_q, mla_w_uq, mla_w_dkv, mla_g_kv, mla_w_uk, mla_w_uv, mla_w_o):
    p = dict(norm_pre=norm_pre, norm_post=norm_post, ffn_wg=ffn_wg, ffn_wu=ffn_wu, ffn_wd=ffn_wd,
             w_in_ab=w_in_ab, w_out_ab=w_out_ab, hg_lb_logits=hg_lb_logits, hg_norm=hg_norm,
             ssm_conv_w=ssm_conv_w, ssm_conv_b=ssm_conv_b, ssm_dt_bias=ssm_dt_bias, ssm_a_log=ssm_a_log,
             ssm_d=ssm_d, ssm_norm=ssm_norm, mla_w_dq=mla_w_dq, mla_g_q=mla_g_q, mla_w_uq=mla_w_uq,
             mla_w_dkv=mla_w_dkv, mla_g_kv=mla_g_kv, mla_w_uk=mla_w_uk, mla_w_uv=mla_w_uv, mla_w_o=mla_w_o)
    bp, seq, d = x_prompt.shape
    db, dseq, _ = x_sample.shape
    assert dseq == 1
    n_pages = page_table.shape[1]
    past_len = n_pages * PAGE_SIZE
    f = ffn_wg.shape[-1]

    head_of = np.arange(SSM_INNER) // SSM_HEAD_DIM
    emat = jnp.asarray((np.arange(LANES)[:, None] == head_of[None, :]).astype(np.float32))
    consts = dict(emat=emat)

    mod_all = _ada_mod(jnp.concatenate([c_prompt, c_sample], axis=0), ada_w, ada_b)

    fb = 2 * LANES if f % (2 * LANES) == 0 else f
    w_lo = _prep_weights(p, BF16)
    cfg_p = dict(tm_ffn=512, fb=fb, tm_mm=512, l_pad=seq, hg_chunk=min(64, seq), hg_inner=max(1, min(4, seq // 64)),
                 hg_hps=1,
                 ssd_chunk=min(256, seq), tq=min(128, seq), tk=min(512, seq), tkc=min(512, seq), hc=2)
    hg0 = jnp.zeros((bp, HG_HEADS, HG_KEY, HG_VAL), F32)
    ssm0 = jnp.zeros((bp, SSM_HEADS, SSM_STATE, SSM_HEAD_DIM), F32)
    conv0 = jnp.zeros((bp, SSM_CONV - 1, 2 * SSM_INNER), F32)
    pos_p = jnp.tile(jnp.arange(seq, dtype=F32), bp)
    y_p, hg_p, ssm_p, conv_p, ckv_p, kpe_p = _trunk(
        x_prompt.reshape(bp * seq, d), mod_all[:, :bp], pos_p, hg0, ssm0, conv0, None, p, w_lo, consts,
        bsz=bp, seq_len=seq, cdt=BF16, cfg=cfg_p)

    w_hi = _prep_weights(p, F32)
    pg = 16 if n_pages % 16 == 0 else n_pages
    cfg_s = dict(tm_ffn=128, fb=fb, tm_mm=128, l_pad=SUBLANES, hg_chunk=SUBLANES, hg_inner=1, hg_hps=HG_HEADS,
                 ssd_chunk=SUBLANES, pg=pg)
    pos_s = jnp.full((db,), past_len, F32)
    y_s, hg_s, ssm_s, conv_s, ckv_s, kpe_s = _trunk(
        x_sample.reshape(db, d), mod_all[:, bp:], pos_s, state_hgrn, state_ssm, state_conv,
        (cache_ckv, cache_kpe, page_table), p, w_hi, consts, bsz=db, seq_len=1, cdt=F32, cfg=cfg_s)

    return (y_p.reshape(bp, seq, d), y_s.reshape(db, 1, d), hg_p, hg_s, ssm_p, ssm_s, conv_p, conv_s,
            ckv_p.reshape(bp, seq, -1), ckv_s.reshape(db, 1, -1), kpe_p.reshape(bp, seq, -1),
            kpe_s.reshape(db, 1, -1))
```

```python
import functools
import math

import jax
import jax.numpy as jnp
import numpy as np
from jax import lax
from jax.experimental import pallas as pl
from jax.experimental.pallas import tpu as pltpu

F32 = jnp.float32
BF16 = jnp.bfloat16
HI = lax.Precision.HIGHEST

EPS = 1e-6
MACARON_W = 0.5
ROPE_THETA = 10000.0
PAGE_SIZE = 128

LANES = 128
SUBLANES = 8
VMEM_LIMIT = 56 * 1024 * 1024

HG_HEADS = 8
HG_KEY = 128
HG_VAL = 128
SSM_HEADS = 16
SSM_HEAD_DIM = 64
SSM_GROUPS = 4
SSM_STATE = 128
SSM_CONV = 4
SSM_INNER = SSM_HEADS * SSM_HEAD_DIM
SSM_GROUP_W = SSM_INNER // SSM_GROUPS
HEADS_PER_GROUP = SSM_HEADS // SSM_GROUPS
MLA_HEADS = 16
MLA_Q_RANK = 512
MLA_KV_RANK = 256
MLA_NOPE = 128
MLA_ROPE = 64
MLA_V = 128
MLA_SCALE = (MLA_NOPE + MLA_ROPE) ** -0.5
DECODE_SUB_PAGES = 8
FFN_AHEAD = 2
MLA_Q_AHEAD = 2
ATTN_AHEAD = 2
HG_SUB = SUBLANES


def _cparams(sem):
    return pltpu.CompilerParams(dimension_semantics=sem, vmem_limit_bytes=VMEM_LIMIT)


def _dot(a, b, cdt):
    if cdt == F32:
        return jnp.dot(a.astype(F32), b.astype(F32), precision=HI, preferred_element_type=F32)
    return jnp.dot(a.astype(BF16), b.astype(BF16), preferred_element_type=F32)


def _dot_nt(a, b, cdt):
    dn = (((1,), (1,)), ((), ()))
    if cdt == F32:
        return lax.dot_general(a.astype(F32), b.astype(F32), dn, precision=HI, preferred_element_type=F32)
    return lax.dot_general(a.astype(BF16), b.astype(BF16), dn, preferred_element_type=F32)


def _dot_hi(a, b):
    return jnp.dot(a, b, precision=HI, preferred_element_type=F32)


def _rms(x, g):
    r = lax.rsqrt(jnp.mean(x * x, axis=-1, keepdims=True) + EPS)
    return (x * r) * g


def _sigmoid(x):
    return 1.0 / (1.0 + jnp.exp(-x))


def _silu(x):
    return x * _sigmoid(x)


def _cumsum_rows(x):
    n = x.shape[0]
    row = lax.broadcasted_iota(jnp.int32, (n, 1), 0)
    s = 1
    while s < n:
        x = x + jnp.where(row >= s, pltpu.roll(x, s, 0), 0.0)
        s *= 2
    return x


def _tile(n, pref):
    t = min(n, pref)
    while n % t:
        t -= SUBLANES
    return t


def _mod_spec(per_row, tm, tiles_per_seq, d):
    if per_row:
        return pl.BlockSpec((1, tm, d), lambda i, *_: (0, i, 0))
    return pl.BlockSpec((1, 1, d), lambda i, *_: (i // tiles_per_seq, 0, 0))


def _ada_kernel(c_ref, w_ref, b_ref, o_ref):
    cs = _silu(c_ref[...])
    o_ref[0] = _dot_hi(cs, w_ref[0]) + b_ref[0]


def _ada_mod(c_all, ada_w, ada_b):
    depth, d, n = ada_w.shape
    m = c_all.shape[0]
    tn = _tile(n, 1152)
    return pl.pallas_call(
        _ada_kernel,
        grid=(depth, n // tn),
        in_specs=[pl.BlockSpec((m, d), lambda l, j: (0, 0)),
                  pl.BlockSpec((1, d, tn), lambda l, j: (l, 0, j)),
                  pl.BlockSpec((1, 1, tn), lambda l, j: (l, 0, j))],
        out_specs=pl.BlockSpec((1, m, tn), lambda l, j: (l, 0, j)),
        out_shape=jax.ShapeDtypeStruct((depth, m, n), F32),
        compiler_params=_cparams(("parallel", "parallel")),
        name="ada_mod",
    )(c_all, ada_w, ada_b.reshape(depth, 1, n))


def _ffn_kernel(x_ref, gpre_ref, sh_ref, sc_ref, gt_ref, gpost_ref, wg_ref, wu_ref, wd_ref,
                o_ref, a_scr, *, cdt, fb):
    x = x_ref[...]
    h = (_rms(x, gpre_ref[...]) * (1.0 + sc_ref[0]) + sh_ref[0]).astype(cdt)
    nb = wg_ref.shape[1] // fb

    def gate_up(s):
        fs = slice(s * fb, (s + 1) * fb)
        return _dot(h, wg_ref[:, fs], cdt), _dot(h, wu_ref[:, fs], cdt)

    ahead = [gate_up(s) for s in range(min(FFN_AHEAD, nb))]
    for s in range(nb):
        g, u = ahead.pop(0)
        if s + FFN_AHEAD < nb:
            ahead.append(gate_up(s + FFN_AHEAD))
        a_scr[:, s * fb:(s + 1) * fb] = (_silu(g) * u).astype(cdt)
    y = _dot(a_scr[...], wd_ref[...], cdt)
    o_ref[...] = x + MACARON_W * gt_ref[0] * _rms(y, gpost_ref[...])


def _ffn(x, mods, gpre, gpost, wg, wu, wd, *, per_row, seq_len, cdt, tm_pref, fb):
    n, d = x.shape
    f = wg.shape[1]
    tm = _tile(seq_len if not per_row else n, tm_pref)
    tps = max(seq_len // tm, 1)
    shift, scale, gate = mods
    mspec = _mod_spec(per_row, tm, tps, d)
    vec = pl.BlockSpec((1, d), lambda i: (0, 0))
    whole = lambda a: pl.BlockSpec(a.shape, lambda i: (0, 0), pipeline_mode=pl.Buffered(1))
    return pl.pallas_call(
        functools.partial(_ffn_kernel, cdt=cdt, fb=fb),
        grid=(n // tm,),
        in_specs=[pl.BlockSpec((tm, d), lambda i: (i, 0)), vec, mspec, mspec, mspec, vec,
                  whole(wg), whole(wu), whole(wd)],
        out_specs=pl.BlockSpec((tm, d), lambda i: (i, 0)),
        out_shape=jax.ShapeDtypeStruct((n, d), F32),
        scratch_shapes=[pltpu.VMEM((tm, f), cdt)],
        compiler_params=_cparams(("parallel",)),
        name="ffn",
    )(x, gpre, shift, scale, gate, gpost, wg, wu, wd)


def _modmm_kernel(x_ref, gpre_ref, sh_ref, sc_ref, w_ref, wx_ref, o_ref, ox_ref, *, cdt, tn):
    h = (_rms(x_ref[...], gpre_ref[...]) * (1.0 + sc_ref[0]) + sh_ref[0]).astype(cdt)
    ox_ref[...] = _dot(h, wx_ref[...], cdt)
    for jb in range(w_ref.shape[1] // tn):
        cs = slice(jb * tn, (jb + 1) * tn)
        o_ref[:, cs] = _dot(h, w_ref[:, cs], cdt)


def _modmm(x, mods, gpre, w, wx, *, per_row, seq_len, cdt, tm_pref, tn):
    n, d = x.shape
    nout = w.shape[1]
    nx = wx.shape[1]
    tm = _tile(seq_len if not per_row else n, tm_pref)
    tps = max(seq_len // tm, 1)
    shift, scale, _ = mods
    mspec = _mod_spec(per_row, tm, tps, d)
    whole = lambda a: pl.BlockSpec(a.shape, lambda i: (0, 0), pipeline_mode=pl.Buffered(1))
    return pl.pallas_call(
        functools.partial(_modmm_kernel, cdt=cdt, tn=tn),
        grid=(n // tm,),
        in_specs=[pl.BlockSpec((tm, d), lambda i: (i, 0)),
                  pl.BlockSpec((1, d), lambda i: (0, 0)), mspec, mspec, whole(w), whole(wx)],
        out_specs=[pl.BlockSpec((tm, nout), lambda i: (i, 0)),
                   pl.BlockSpec((tm, nx), lambda i: (i, 0))],
        out_shape=[jax.ShapeDtypeStruct((n, nout), F32), jax.ShapeDtypeStruct((n, nx), F32)],
        compiler_params=_cparams(("parallel",)),
        name="modmm",
    )(x, gpre, shift, scale, w, wx)


def _hgrn_kernel(q_ref, f_ref, v_ref, g_ref, lb_ref, gn_ref, s0_ref, o_ref, sout_ref, st_scr,
                 *, chunk, n_inner, hps, l_true, nl, cdt):
    l = pl.program_id(2)
    t_blk = chunk * n_inner
    sub = HG_SUB
    nsub = chunk // sub

    @pl.when(l == 0)
    def _():
        for hh in range(hps):
            st_scr[hh] = s0_ref[0, hh].T

    rowc = lax.broadcasted_iota(jnp.int32, (chunk, 1), 0)
    tril = (lax.broadcasted_iota(jnp.int32, (chunk, chunk), 0)
            >= lax.broadcasted_iota(jnp.int32, (chunk, chunk), 1)).astype(F32)
    srow = lax.broadcasted_iota(jnp.int32, (nsub, sub, 1), 1)
    blocks = lambda a: a.reshape(nsub, sub, a.shape[-1])

    for hh in range(hps):
        cols = slice(hh * HG_KEY, (hh + 1) * HG_KEY)
        lbh = lb_ref[:, cols]
        gn = gn_ref[:, cols]
        pre = []
        for c in range(n_inner):
            rows = slice(c * chunk, (c + 1) * chunk)
            valid = (l * t_blk + c * chunk + rowc) < l_true
            fr = f_ref[rows, cols]
            v = v_ref[rows, cols]
            f = lbh + (1.0 - lbh) * _sigmoid(fr)
            logf = jnp.where(valid, jnp.log(f), 0.0)
            k = jnp.where(valid, (1.0 - lbh) * _sigmoid(-fr), 0.0)
            q = _silu(q_ref[rows, cols])
            b = _dot_hi(tril, logf)
            pre.append((rows, q, k, v, b))
        atts = []
        for rows, q, k, v, b in pre:
            att_c = []
            for i in range(1, nsub):
                lo = i * sub
                hi = min(chunk, -(-lo // 16) * 16)
                ref_b = b[lo - 1:lo, :]
                qi = q[lo:lo + sub, :] * jnp.exp(b[lo:lo + sub, :] - ref_b)
                ki = jnp.where(rowc[0:hi] < lo, k[0:hi, :] * jnp.exp(ref_b - b[0:hi, :]), 0.0)
                att_c.append((_dot_nt(qi, ki, cdt), hi))
            atts.append(att_c)
        intra, incs = [], []
        for (rows, q, k, v, b), att_c in zip(pre, atts):
            q3, k3, b3, v3 = blocks(q), blocks(k), blocks(b), blocks(v)
            o3 = jnp.sum(q3 * k3, axis=-1, keepdims=True) * v3
            for d in range(1, sub):
                k_sh = pltpu.roll(k3, d, 1)
                b_sh = pltpu.roll(b3, d, 1)
                v_sh = pltpu.roll(v3, d, 1)
                a = jnp.sum(q3 * k_sh * jnp.exp(b3 - b_sh), axis=-1, keepdims=True)
                o3 = o3 + jnp.where(srow >= d, a, 0.0) * v_sh
            o = o3.reshape(chunk, HG_VAL)
            if nsub > 1:
                parts = [jnp.zeros((sub, HG_VAL), F32)] + [_dot(att, v[0:hi, :], cdt) for att, hi in att_c]
                o = o + jnp.concatenate(parts, axis=0)
            intra.append(o)
            bl = b[chunk - 1:chunk, :]
            incs.append(_dot(v.T, k * jnp.exp(bl - b), cdt))
        st = st_scr[hh]
        for (rows, q, k, v, b), o, inc in zip(pre, intra, incs):
            o = o + _dot_nt(q * jnp.exp(b), st, cdt)
            st = st * jnp.exp(b[chunk - 1:chunk, :]) + inc
            o_ref[rows, cols] = _rms(o, gn) * _silu(g_ref[rows, cols])
        st_scr[hh] = st

    @pl.when(l == nl - 1)
    def _():
        for hh in range(hps):
            sout_ref[0, hh] = st_scr[hh].T


def _hgrn(proj, lb, gn, s0, *, bsz, l_pad, l_true, chunk, n_inner, hps, cdt):
    t_blk = chunk * n_inner
    nl = l_pad // t_blk
    n = bsz * l_pad
    nhb = HG_HEADS // hps
    wide = hps * HG_KEY

    def col(cb):
        return pl.BlockSpec((t_blk, wide), lambda b, h, l: (b * nl + l, cb * nhb + h))

    hvec = pl.BlockSpec((1, wide), lambda b, h, l: (0, h))
    sspec = pl.BlockSpec((1, hps, HG_KEY, HG_VAL), lambda b, h, l: (b, h, 0, 0))
    return pl.pallas_call(
        functools.partial(_hgrn_kernel, chunk=chunk, n_inner=n_inner, hps=hps, l_true=l_true, nl=nl, cdt=cdt),
        grid=(bsz, nhb, nl),
        in_specs=[col(0), col(1), col(2), col(3), hvec, hvec, sspec],
        out_specs=[pl.BlockSpec((t_blk, wide), lambda b, h, l: (b * nl + l, h)), sspec],
        out_shape=[jax.ShapeDtypeStruct((n, HG_HEADS * HG_VAL), F32),
                   jax.ShapeDtypeStruct(s0.shape, F32)],
        scratch_shapes=[pltpu.VMEM((hps, HG_VAL, HG_KEY), F32)],
        compiler_params=_cparams(("parallel", "parallel", "arbitrary")),
        name="hgrn2",
    )(proj, proj, proj, proj, lb, gn, s0)


def _ssd_kernel(z_ref, xa_ref, xb_ref, dt_ref, cw_ref, cb_ref, dtb_ref, alog_ref, dexp_ref, nrm_ref,
                e_ref, cprev_ref, s0_ref, y_ref, sout_ref, xpad_scr, s_scr, *, chunk, l_true, nl, cdt):
    l = pl.program_id(1)
    c = chunk
    gw = SSM_GROUP_W

    @pl.when(l == 0)
    def _():
        xpad_scr[0:SUBLANES, :] = cprev_ref[0]
        for g in range(SSM_GROUPS):
            s_scr[g] = s0_ref[0, g * HEADS_PER_GROUP:(g + 1) * HEADS_PER_GROUP].reshape(gw, SSM_STATE)

    @pl.when(l > 0)
    def _():
        xpad_scr[0:SUBLANES, :] = xpad_scr[c:c + SUBLANES, :]

    xpad_scr[SUBLANES:SUBLANES + c, 0:SSM_INNER] = xa_ref[...]
    xpad_scr[SUBLANES:SUBLANES + c, SSM_INNER:2 * SSM_INNER] = xb_ref[...]
    conv = cb_ref[...]
    for i in range(SSM_CONV):
        lo = SUBLANES - (SSM_CONV - 1) + i
        conv = conv + cw_ref[i:i + 1, :] * xpad_scr[lo:lo + c, :]
    xc = _silu(conv)
    xh = xc[:, 0:SSM_INNER]
    bm = xc[:, SSM_INNER:SSM_INNER + SSM_GROUPS * SSM_STATE]
    cm = xc[:, SSM_INNER + SSM_GROUPS * SSM_STATE:]

    rowc = lax.broadcasted_iota(jnp.int32, (c, 1), 0)
    valid = (l * c + rowc) < l_true
    draw = dt_ref[...] + dtb_ref[...]
    dt = jnp.maximum(draw, 0.0) + jnp.log1p(jnp.exp(-jnp.abs(draw)))
    dt = jnp.where(valid, dt, 0.0)
    la = dt * (-jnp.exp(alog_ref[...]))
    ri = lax.broadcasted_iota(jnp.int32, (c, c), 0)
    ci = lax.broadcasted_iota(jnp.int32, (c, c), 1)
    causal = ri >= ci
    b = _cumsum_rows(la)
    b_t = b.T
    dt_t = dt.T
    bl = b[c - 1:c, :]
    e = e_ref[...]

    def expand(a):
        hi = a.astype(BF16)
        r1 = a - hi.astype(F32)
        mid = r1.astype(BF16)
        lo = (r1 - mid.astype(F32)).astype(BF16)
        return (jnp.dot(hi, e, preferred_element_type=F32) + jnp.dot(mid, e, preferred_element_type=F32)
                + jnp.dot(lo, e, preferred_element_type=F32))

    lane = lax.broadcasted_iota(jnp.int32, (1, gw), 1)
    grp = lambda a, g: a[:, g * SSM_STATE:(g + 1) * SSM_STATE]
    gsl = lambda g: slice(g * gw, (g + 1) * gw)
    cbs = [_dot_nt(grp(cm, g), grp(bm, g), cdt) for g in range(SSM_GROUPS)]
    carried = [_dot_nt(grp(cm, g), s_scr[g], cdt) for g in range(SSM_GROUPS)]
    eb_e = expand(jnp.exp(b))
    dtw_e = expand(dt * jnp.exp(bl - b))
    for g in range(SSM_GROUPS):
        gs = gsl(g)
        decay = jnp.broadcast_to(eb_e[c - 1:c, gs], (SUBLANES, gw)).T[:, 0:1]
        s_scr[g] = decay * s_scr[g] + _dot((xh[:, gs] * dtw_e[:, gs]).T, grp(bm, g), cdt)
    ys = []
    for g in range(SSM_GROUPS):
        gs = gsl(g)
        yg = carried[g] * eb_e[:, gs]
        for j in range(HEADS_PER_GROUP):
            h = g * HEADS_PER_GROUP + j
            seg = jnp.exp(jnp.where(causal, b[:, h:h + 1] - b_t[h:h + 1, :], -jnp.inf)) * dt_t[h:h + 1, :]
            in_head = (lane >= j * SSM_HEAD_DIM) & (lane < (j + 1) * SSM_HEAD_DIM)
            yg = yg + _dot(cbs[g] * seg, jnp.where(in_head, xh[:, gs], 0.0), cdt)
        ys.append(yg)
    y = jnp.concatenate(ys, axis=-1)
    y = (y + dexp_ref[...] * xh) * _silu(z_ref[...])
    outs = []
    for g in range(SSM_GROUPS):
        gs = slice(g * gw, (g + 1) * gw)
        outs.append(_rms(y[:, gs], nrm_ref[:, gs]))
    y_ref[...] = jnp.concatenate(outs, axis=-1)

    @pl.when(l == nl - 1)
    def _():
        for g in range(SSM_GROUPS):
            sout_ref[0, g * HEADS_PER_GROUP:(g + 1) * HEADS_PER_GROUP] = s_scr[g].reshape(
                HEADS_PER_GROUP, SSM_HEAD_DIM, SSM_STATE)


def _ssd(proj, dtraw, cw, cb, dtb, alog, dexp, nrm, emat, cprev, s0g, *, bsz, l_pad, l_true, chunk, cdt):
    nl = l_pad // chunk
    n = bsz * l_pad
    w = SSM_INNER
    zcol = 4 * HG_HEADS * HG_KEY // w
    cc = 2 * w

    def col(cbk):
        return pl.BlockSpec((chunk, w), lambda b, l: (b * nl + l, cbk))

    def const(shape):
        return pl.BlockSpec(shape, lambda b, l: (0,) * len(shape))

    sspec = pl.BlockSpec((1, SSM_HEADS, SSM_HEAD_DIM, SSM_STATE), lambda b, l: (b, 0, 0, 0))
    return pl.pallas_call(
        functools.partial(_ssd_kernel, chunk=chunk, l_true=l_true, nl=nl, cdt=cdt),
        grid=(bsz, nl),
        in_specs=[col(zcol), col(zcol + 1), col(zcol + 2),
                  pl.BlockSpec((chunk, LANES), lambda b, l: (b * nl + l, 0)),
                  const((SSM_CONV, cc)), const((1, cc)), const((1, LANES)), const((1, LANES)),
                  const((1, w)), const((1, w)), const((LANES, w)),
                  pl.BlockSpec((1, SUBLANES, cc), lambda b, l: (b, 0, 0)), sspec],
        out_specs=[pl.BlockSpec((chunk, w), lambda b, l: (b * nl + l, 0)), sspec],
        out_shape=[jax.ShapeDtypeStruct((n, w), F32), jax.ShapeDtypeStruct(s0g.shape, F32)],
        scratch_shapes=[pltpu.VMEM((chunk + 2 * SUBLANES, cc), F32),
                        pltpu.VMEM((SSM_GROUPS, SSM_GROUP_W, SSM_STATE), F32)],
        compiler_params=_cparams(("parallel", "arbitrary")),
        name="ssd",
    )(proj, proj, proj, dtraw, cw, cb, dtb, alog, dexp, nrm, emat, cprev, s0g)


def _outproj_kernel(x_ref, a_ref, b_ref, w_ref, gt_ref, gpost_ref, o_ref, *, cdt):
    ka = a_ref.shape[1]
    y = _dot(a_ref[...], w_ref[0:ka, :], cdt) + _dot(b_ref[...], w_ref[ka:, :], cdt)
    o_ref[...] = x_ref[...] + gt_ref[0] * _rms(y, gpost_ref[...])


def _outproj(x, a, b, w, gate, gpost, *, per_row, seq_len, cdt, tm_pref):
    n, d = x.shape
    tm = _tile(seq_len if not per_row else n, tm_pref)
    tps = max(seq_len // tm, 1)
    mspec = _mod_spec(per_row, tm, tps, d)
    row = lambda width: pl.BlockSpec((tm, width), lambda i: (i, 0))
    return pl.pallas_call(
        functools.partial(_outproj_kernel, cdt=cdt),
        grid=(n // tm,),
        in_specs=[row(d), row(a.shape[1]), row(b.shape[1]),
                  pl.BlockSpec(w.shape, lambda i: (0, 0)), mspec,
                  pl.BlockSpec((1, d), lambda i: (0, 0))],
        out_specs=row(d),
        out_shape=jax.ShapeDtypeStruct((n, d), F32),
        compiler_params=_cparams(("parallel",)),
        name="outproj",
    )(x, a, b, w, gate, gpost)


def _rope(xs, cs_t):
    u = xs * cs_t
    return u + pltpu.roll(u, MLA_ROPE, 1)


def _mlaq_kernel(qr_ref, kv_ref, gq_ref, gkv_ref, wuq_ref, wuk_ref, cs_ref,
                 qx_ref, ckv_ref, kpe_ref, kx_ref, kxt_ref, *, cdt, q_scale):
    cs_t = cs_ref[...]
    lane = lax.broadcasted_iota(jnp.int32, (1, 2 * MLA_ROPE), 1)
    low = lane < MLA_ROPE
    cq = _rms(qr_ref[...], gq_ref[...]).astype(cdt)
    kv = kv_ref[...]
    ckv = _rms(kv[:, 0:MLA_KV_RANK], gkv_ref[...])
    kpe = _rope(kv[:, MLA_KV_RANK:MLA_KV_RANK + 2 * MLA_ROPE], cs_t)
    ckv_ref[...] = ckv
    kpe_ref[...] = kpe[:, 0:MLA_ROPE]
    kx = jnp.concatenate([ckv, jnp.where(low, kpe, 0.0)], axis=-1)
    kx_ref[...] = kx.astype(kx_ref.dtype)
    kxt_ref[...] = kx.T.astype(kxt_ref.dtype)
    up = lambda h: _dot(cq, wuq_ref[h], cdt)
    ahead = [up(h) for h in range(MLA_Q_AHEAD)]
    for h in range(MLA_HEADS):
        q = ahead.pop(0)
        if h + MLA_Q_AHEAD < MLA_HEADS:
            ahead.append(up(h + MLA_Q_AHEAD))
        ql = _dot(q[:, 0:MLA_NOPE], wuk_ref[h], cdt)
        qp = _rope(q[:, MLA_NOPE:MLA_NOPE + 2 * MLA_ROPE], cs_t)
        qx_ref[h, :, 0:MLA_KV_RANK] = (ql * q_scale).astype(qx_ref.dtype)
        qx_ref[h, :, MLA_KV_RANK:] = jnp.where(low, qp * q_scale, 0.0).astype(qx_ref.dtype)


def _mlaq(qraw, kvraw, gq, gkv, wuq, wuk, cs_t, *, cdt, tm_pref, q_scale):
    n = qraw.shape[0]
    tm = _tile(n, tm_pref)
    h = MLA_HEADS
    kw = MLA_KV_RANK + 2 * MLA_ROPE
    row = lambda width: pl.BlockSpec((tm, width), lambda i: (i, 0))
    const = lambda shape: pl.BlockSpec(shape, lambda i: (0,) * len(shape))
    return pl.pallas_call(
        functools.partial(_mlaq_kernel, cdt=cdt, q_scale=q_scale),
        grid=(n // tm,),
        in_specs=[row(MLA_Q_RANK), row(kvraw.shape[1]), const((1, MLA_Q_RANK)), const((1, MLA_KV_RANK)),
                  const(wuq.shape), const(wuk.shape), row(2 * MLA_ROPE)],
        out_specs=[pl.BlockSpec((h, tm, kw), lambda i: (0, i, 0)), row(MLA_KV_RANK), row(MLA_ROPE), row(kw),
                   pl.BlockSpec((kw, tm), lambda i: (0, i))],
        out_shape=[jax.ShapeDtypeStruct((h, n, kw), cdt),
                   jax.ShapeDtypeStruct((n, MLA_KV_RANK), F32),
                   jax.ShapeDtypeStruct((n, MLA_ROPE), F32),
                   jax.ShapeDtypeStruct((n, kw), BF16),
                   jax.ShapeDtypeStruct((kw, n), BF16)],
        compiler_params=_cparams(("parallel",)),
        name="mla_q",
    )(qraw, kvraw, gq, gkv, wuq, wuk, cs_t)


def _attn_kernel(it_ref, jt_ref, qx_ref, kx_ref, kxt_ref, o_ref, m_scr, l_scr, acc_scr, *, tq, tk, hc):
    step = pl.program_id(1)
    i = it_ref[step]
    j = jt_ref[step]
    last = (i * tq + tq - 1) // tk

    @pl.when(j == 0)
    def _():
        m_scr[...] = jnp.full_like(m_scr, -jnp.inf)
        l_scr[...] = jnp.zeros_like(l_scr)
        acc_scr[...] = jnp.zeros_like(acc_scr)

    rc = hc * tq
    kw = qx_ref.shape[2]

    def key_tile(masked):
        kc = kx_ref[:, 0:MLA_KV_RANK]
        if masked:
            qpos = i * tq + (lax.broadcasted_iota(jnp.int32, (rc, tk), 0) & (tq - 1))
            kpos = j * tk + lax.broadcasted_iota(jnp.int32, (rc, tk), 1)
            keep = kpos <= qpos
        logits = lambda c: _dot(qx_ref[c * hc:(c + 1) * hc].reshape(rc, kw), kxt_ref[...], BF16)
        nblk = MLA_HEADS // hc
        ahead = [logits(c) for c in range(min(ATTN_AHEAD, nblk))]
        for c in range(nblk):
            rs = slice(c * rc, (c + 1) * rc)
            s = ahead.pop(0)
            if c + ATTN_AHEAD < nblk:
                ahead.append(logits(c + ATTN_AHEAD))
            if masked:
                s = jnp.where(keep, s, -jnp.inf)
            m_prev = m_scr[rs]
            m_new = jnp.maximum(m_prev, jnp.max(s, axis=-1, keepdims=True))
            alpha = jnp.exp2(m_prev - m_new)
            p = jnp.exp2(s - m_new)
            l_scr[rs] = alpha * l_scr[rs] + jnp.sum(p, axis=-1, keepdims=True)
            acc_scr[rs] = alpha * acc_scr[rs] + _dot(p, kc, BF16)
            m_scr[rs] = m_new

    @pl.when(j < last)
    def _():
        key_tile(False)

    @pl.when(j == last)
    def _():
        key_tile(True)
        o = acc_scr[...] / l_scr[...]
        o_ref[...] = o.reshape(MLA_HEADS, tq, MLA_KV_RANK).astype(o_ref.dtype)


def _attn(qx, kx, kxt, *, bsz, seq_len, tq, tk, hc):
    h, n, kw = qx.shape
    r = MLA_KV_RANK
    nq = seq_len // tq
    nk = seq_len // tk
    rows = h * tq
    pairs = [(i, j) for i in range(nq) for j in range((i * tq + tq - 1) // tk + 1)]
    i_of = jnp.asarray([ij[0] for ij in pairs], jnp.int32)
    j_of = jnp.asarray([ij[1] for ij in pairs], jnp.int32)
    grid_spec = pltpu.PrefetchScalarGridSpec(
        num_scalar_prefetch=2,
        grid=(bsz, len(pairs)),
        in_specs=[pl.BlockSpec((h, tq, kw), lambda b, s, it, jt: (0, b * nq + it[s], 0)),
                  pl.BlockSpec((tk, kw), lambda b, s, it, jt: (b * nk + jt[s], 0)),
                  pl.BlockSpec((kw, tk), lambda b, s, it, jt: (0, b * nk + jt[s]))],
        out_specs=pl.BlockSpec((h, tq, r), lambda b, s, it, jt: (0, b * nq + it[s], 0)),
        scratch_shapes=[pltpu.VMEM((rows, 1), F32), pltpu.VMEM((rows, 1), F32), pltpu.VMEM((rows, r), F32)],
    )
    return pl.pallas_call(
        functools.partial(_attn_kernel, tq=tq, tk=tk, hc=hc),
        grid_spec=grid_spec,
        out_shape=jax.ShapeDtypeStruct((h, n, r), BF16),
        compiler_params=_cparams(("parallel", "arbitrary")),
        name="mla_attn",
    )(i_of, j_of, qx, kx, kxt)


def _decode_kernel(pt_ref, qx_ref, ckn_ref, kpn_ref, cache_c, cache_p, o_ref, cbuf, pbuf, sems,
                   *, n_pages, pg, sub_pg, n_seq):
    bidx = pl.program_id(0)
    n_chunks = n_pages // pg
    total_chunks = n_seq * n_chunks
    sub = sub_pg * PAGE_SIZE

    def page_copies(page, slot, p):
        return (pltpu.make_async_copy(cache_c.at[page], cbuf.at[slot, p], sems.at[0, slot]),
                pltpu.make_async_copy(cache_p.at[page], pbuf.at[slot, p], sems.at[1, slot]))

    def start(g, slot):
        for p in range(pg):
            for cp in page_copies(pt_ref[g * pg + p], slot, p):
                cp.start()

    def wait(slot):
        for p in range(pg):
            for cp in page_copies(0, slot, p):
                cp.wait()

    qx = qx_ref[0]
    ql = qx[:, 0:MLA_KV_RANK]
    qp = qx[:, MLA_KV_RANK:MLA_KV_RANK + MLA_ROPE]

    @pl.when(bidx == 0)
    def _():
        start(0, 0)

    def body(cidx, carry):
        m_prev, l_prev, acc = carry
        g = bidx * n_chunks + cidx
        slot = g % 2

        @pl.when(g + 1 < total_chunks)
        def _():
            start(g + 1, 1 - slot)

        wait(slot)
        kcs, s_parts = [], []
        for sb in range(pg // sub_pg):
            kc = cbuf[slot, sb * sub_pg:(sb + 1) * sub_pg].reshape(sub, MLA_KV_RANK).astype(BF16)
            kcs.append(kc)
            s_rope = jnp.concatenate(
                [_dot(qp, pbuf[slot, sb * sub_pg + p], BF16) for p in range(sub_pg)], axis=-1)
            s_parts.append(_dot_nt(ql, kc, BF16) + s_rope)
        s = jnp.concatenate(s_parts, axis=-1) * MLA_SCALE
        m_new = jnp.maximum(m_prev, jnp.max(s, axis=-1, keepdims=True))
        alpha = jnp.exp(m_prev - m_new)
        p = jnp.exp(s - m_new)
        l_new = alpha * l_prev + jnp.sum(p, axis=-1, keepdims=True)
        pv = _dot(p[:, 0:sub], kcs[0], BF16)
        for sb in range(1, pg // sub_pg):
            pv = pv + _dot(p[:, sb * sub:(sb + 1) * sub], kcs[sb], BF16)
        return m_new, l_new, alpha * acc + pv

    init = (jnp.full((MLA_HEADS, 1), -jnp.inf, F32), jnp.zeros((MLA_HEADS, 1), F32),
            jnp.zeros((MLA_HEADS, MLA_KV_RANK), F32))
    m_prev, l_prev, acc = lax.fori_loop(0, n_chunks, body, init)
    ckn = ckn_ref[0]
    s_own = (jnp.sum(ql * ckn, axis=-1, keepdims=True)
             + jnp.sum(qp * kpn_ref[0], axis=-1, keepdims=True)) * MLA_SCALE
    m_new = jnp.maximum(m_prev, s_own)
    alpha = jnp.exp(m_prev - m_new)
    p_own = jnp.exp(s_own - m_new)
    l_new = alpha * l_prev + p_own
    o_ref[0] = (alpha * acc + p_own * ckn) / l_new


def _decode(page_table, qx, ckn, kpn, cache_c, cache_pt, *, pg):
    db, n_pages = page_table.shape
    h, r, dr = MLA_HEADS, MLA_KV_RANK, MLA_ROPE
    grid_spec = pltpu.PrefetchScalarGridSpec(
        num_scalar_prefetch=1,
        grid=(db,),
        in_specs=[pl.BlockSpec((1, h, qx.shape[2]), lambda b, pt: (b, 0, 0)),
                  pl.BlockSpec((1, 1, r), lambda b, pt: (b, 0, 0)),
                  pl.BlockSpec((1, 1, dr), lambda b, pt: (b, 0, 0)),
                  pl.BlockSpec(memory_space=pl.ANY),
                  pl.BlockSpec(memory_space=pl.ANY)],
        out_specs=pl.BlockSpec((1, h, r), lambda b, pt: (b, 0, 0)),
        scratch_shapes=[pltpu.VMEM((2, pg, PAGE_SIZE, r), F32),
                        pltpu.VMEM((2, pg, dr, PAGE_SIZE), F32),
                        pltpu.SemaphoreType.DMA((2, 2))],
    )
    return pl.pallas_call(
        functools.partial(_decode_kernel, n_pages=n_pages, pg=pg, sub_pg=min(DECODE_SUB_PAGES, pg), n_seq=db),
        grid_spec=grid_spec,
        out_shape=jax.ShapeDtypeStruct((db, h, r), F32),
        compiler_params=_cparams(("arbitrary",)),
        name="mla_decode",
    )(page_table.reshape(-1), qx, ckn, kpn, cache_c, cache_pt)


def _mlaout_kernel(x_ref, ol_ref, wuv_ref, wo_ref, gt_ref, gpost_ref, o_ref, *, cdt):
    t = jnp.concatenate([_dot(ol_ref[h], wuv_ref[h], cdt).astype(cdt) for h in range(MLA_HEADS)], axis=-1)
    y = _dot(t, wo_ref[...], cdt)
    o_ref[...] = x_ref[...] + gt_ref[0] * _rms(y, gpost_ref[...])


def _mlaout(x, ol, wuv, wo, gate, gpost, *, per_row, seq_len, cdt, tm_pref):
    n, d = x.shape
    tm = _tile(seq_len if not per_row else n, tm_pref)
    tps = max(seq_len // tm, 1)
    mspec = _mod_spec(per_row, tm, tps, d)
    return pl.pallas_call(
        functools.partial(_mlaout_kernel, cdt=cdt),
        grid=(n // tm,),
        in_specs=[pl.BlockSpec((tm, d), lambda i: (i, 0)),
                  pl.BlockSpec((MLA_HEADS, tm, MLA_KV_RANK), lambda i: (0, i, 0)),
                  pl.BlockSpec(wuv.shape, lambda i: (0, 0, 0)),
                  pl.BlockSpec(wo.shape, lambda i: (0, 0)),
                  mspec, pl.BlockSpec((1, d), lambda i: (0, 0))],
        out_specs=pl.BlockSpec((tm, d), lambda i: (i, 0)),
        out_shape=jax.ShapeDtypeStruct((n, d), F32),
        compiler_params=_cparams(("parallel",)),
        name="mla_out",
    )(x, ol, wuv, wo, gate, gpost)


def _prep_weights(p, cdt):
    d = p['w_in_ab'].shape[0]
    n_main = 4 * HG_HEADS * HG_KEY + SSM_INNER + 2 * SSM_INNER
    w_in = p['w_in_ab']
    w_dt = jnp.pad(w_in[:, n_main:], ((0, 0), (0, LANES - SSM_HEADS)))
    exch = lambda a: jnp.concatenate([a, jnp.roll(a[..., -MLA_ROPE:], MLA_ROPE // 2, axis=-1)], axis=-1)
    w_dkv = exch(p['mla_w_dkv'])
    w_uq = exch(p['mla_w_uq'].reshape(MLA_Q_RANK, MLA_HEADS, MLA_NOPE + MLA_ROPE)).transpose(1, 0, 2)
    c = lambda a: a.astype(cdt)
    return dict(
        ffn_wg=c(p['ffn_wg']), ffn_wu=c(p['ffn_wu']), ffn_wd=c(p['ffn_wd']),
        w_in=c(w_in[:, :n_main]), w_dt=c(w_dt), w_out=c(p['w_out_ab']),
        w_dq=c(p['mla_w_dq']), w_dkv=c(w_dkv),
        w_uq=c(w_uq),
        w_uk=c(p['mla_w_uk'].transpose(1, 2, 0)),
        w_uv=c(p['mla_w_uv'].transpose(1, 0, 2)),
        w_o=c(p['mla_w_o']),
    )


def _rope_tables(pos):
    half = MLA_ROPE // 2
    inv = ROPE_THETA ** (-jnp.arange(half, dtype=F32) / half)
    ang = pos[:, None] * inv[None]
    cos, sin = jnp.cos(ang), jnp.sin(ang)
    return jnp.concatenate([cos, cos, -sin, sin], axis=-1)


def _trunk(x, mod_all, pos, hg_s0, ssm_s0, conv_prev, past, p, w, consts, *, bsz, seq_len, cdt, cfg):
    n, d = x.shape
    per_row = seq_len == 1
    vec = lambda a: a.reshape(1, -1)

    def mods(layer, sub):
        m = mod_all[layer].reshape(bsz, 3, 3, d)[:, sub]
        if per_row:
            return tuple(m[:, k].reshape(1, bsz, d) for k in range(3))
        return tuple(m[:, k].reshape(bsz, 1, d) for k in range(3))

    common = dict(per_row=per_row, seq_len=seq_len, cdt=cdt)

    def ffn(x, layer, which, sub):
        return _ffn(x, mods(layer, sub), vec(p['norm_pre'][layer, sub]), vec(p['norm_post'][layer, sub]),
                    w['ffn_wg'][layer, which], w['ffn_wu'][layer, which], w['ffn_wd'][layer, which],
                    tm_pref=cfg['tm_ffn'], fb=cfg['fb'], **common)

    x = ffn(x, 0, 0, 0)
    m1 = mods(0, 1)
    proj, dtraw = _modmm(x, m1, vec(p['norm_pre'][0, 1]), w['w_in'], w['w_dt'],
                         tm_pref=cfg['tm_in'], tn=SSM_INNER, **common)
    l_pad = cfg['l_pad']
    if l_pad != seq_len:
        padrows = lambda a: jnp.pad(a.reshape(bsz, seq_len, -1), ((0, 0), (0, l_pad - seq_len), (0, 0))
                                    ).reshape(bsz * l_pad, -1)
        proj_p, dtraw_p = padrows(proj), padrows(dtraw)
    else:
        proj_p, dtraw_p = proj, dtraw
    lb = jnp.cumsum(jax.nn.softmax(p['hg_lb_logits'].astype(F32), axis=0), axis=0)[0]
    o_a, hg_s = _hgrn(proj_p, vec(lb), vec(p['hg_norm']), hg_s0, bsz=bsz, l_pad=l_pad, l_true=seq_len,
                      chunk=cfg['hg_chunk'], n_inner=cfg['hg_inner'], hps=cfg['hg_hps'], cdt=cdt)
    cprev8 = jnp.pad(conv_prev, ((0, 0), (SUBLANES - (SSM_CONV - 1), 0), (0, 0)))
    pad16 = lambda a: jnp.pad(a.astype(F32), (0, LANES - SSM_HEADS)).reshape(1, LANES)
    y_b, ssm_sg = _ssd(proj_p, dtraw_p, p['ssm_conv_w'], vec(p['ssm_conv_b']), pad16(p['ssm_dt_bias']),
                       pad16(p['ssm_a_log']), vec(jnp.repeat(p['ssm_d'].astype(F32), SSM_HEAD_DIM)),
                       vec(p['ssm_norm']), consts['emat'], cprev8, ssm_s0.transpose(0, 1, 3, 2),
                       bsz=bsz, l_pad=l_pad, l_true=seq_len, chunk=cfg['ssd_chunk'], cdt=cdt)
    ssm_s = ssm_sg.transpose(0, 1, 3, 2)
    if l_pad != seq_len:
        unpad = lambda a: a.reshape(bsz, l_pad, -1)[:, :seq_len].reshape(bsz * seq_len, -1)
        o_a, y_b = unpad(o_a), unpad(y_b)
    n_tail = min(seq_len, SSM_CONV - 1)
    xbc_tail = proj.reshape(bsz, seq_len, -1)[:, seq_len - n_tail:, 5 * SSM_INNER:7 * SSM_INNER]
    conv_s = jnp.concatenate([conv_prev, xbc_tail], axis=1)[:, -(SSM_CONV - 1):]
    x = _outproj(x, o_a, y_b, w['w_out'], m1[2], vec(p['norm_post'][0, 1]), tm_pref=cfg['tm_mm'], **common)
    x = ffn(x, 0, 1, 2)

    x = ffn(x, 1, 0, 0)
    m1 = mods(1, 1)
    qraw, kvraw = _modmm(x, m1, vec(p['norm_pre'][1, 1]), w['w_dq'], w['w_dkv'],
                         tm_pref=cfg['tm_mm'], tn=MLA_Q_RANK, **common)
    q_scale = 1.0 if past is not None else MLA_SCALE * math.log2(math.e)
    qx, ckv, kpe, kx, kxt = _mlaq(qraw, kvraw, vec(p['mla_g_q']), vec(p['mla_g_kv']), w['w_uq'], w['w_uk'],
                                  pos, cdt=cdt, tm_pref=cfg['tm_mm'], q_scale=q_scale)
    if past is None:
        ol = _attn(qx, kx, kxt, bsz=bsz, seq_len=seq_len, tq=cfg['tq'], tk=cfg['tk'], hc=cfg['hc'])
    else:
        cache_c, cache_p, page_table = past
        o_dec = _decode(page_table, qx.transpose(1, 0, 2), ckv.reshape(bsz, 1, -1), kpe.reshape(bsz, 1, -1),
                        cache_c, cache_p.transpose(0, 2, 1), pg=cfg['pg'])
        ol = o_dec.transpose(1, 0, 2)
    x = _mlaout(x, ol, w['w_uv'], w['w_o'], m1[2], vec(p['norm_post'][1, 1]), tm_pref=cfg['tm_mm'], **common)
    x = ffn(x, 1, 1, 2)
    return x, hg_s, ssm_s, conv_s, ckv, kpe


def kernel(x_prompt, x_sample, c_prompt, c_sample, state_hgrn, state_ssm, state_conv, cache_ckv, cache_kpe, page_table, ada_w, ada_b, norm_pre, norm_post, ffn_wg, ffn_wu, ffn_wd, w_in_ab, w_out_ab, hg_lb_logits, hg_norm, ssm_conv_w, ssm_conv_b, ssm_dt_bias, ssm_a_log, ssm_d, ssm_norm, mla_w_dq, mla_g_q, mla_w_uq, mla_w_dkv, mla_g_kv, mla_w_uk, mla_w_uv, mla_w_o):
    p = dict(norm_pre=norm_pre, norm_post=norm_post, ffn_wg=ffn_wg, ffn_wu=ffn_wu, ffn_wd=ffn_wd,
             w_in_ab=w_in_ab, w_out_ab=w_out_ab, hg_lb_logits=hg_lb_logits, hg_norm=hg_norm,
             ssm_conv_w=ssm_conv_w, ssm_conv_b=ssm_conv_b, ssm_dt_bias=ssm_dt_bias, ssm_a_log=ssm_a_log,
             ssm_d=ssm_d, ssm_norm=ssm_norm, mla_w_dq=mla_w_dq, mla_g_q=mla_g_q, mla_w_uq=mla_w_uq,
             mla_w_dkv=mla_w_dkv, mla_g_kv=mla_g_kv, mla_w_uk=mla_w_uk, mla_w_uv=mla_w_uv, mla_w_o=mla_w_o)
    bp, seq, d = x_prompt.shape
    db, dseq, _ = x_sample.shape
    assert dseq == 1
    n_pages = page_table.shape[1]
    past_len = n_pages * PAGE_SIZE
    f = ffn_wg.shape[-1]

    head_of = np.arange(SSM_INNER) // SSM_HEAD_DIM
    emat = jnp.asarray((np.arange(LANES)[:, None] == head_of[None, :]).astype(np.float32), dtype=BF16)
    consts = dict(emat=emat)

    mod_all = _ada_mod(jnp.concatenate([c_prompt, c_sample], axis=0), ada_w, ada_b)

    fb = 2 * LANES if f % (2 * LANES) == 0 else f
    w_lo = _prep_weights(p, BF16)
    cfg_p = dict(tm_ffn=512, fb=fb, tm_mm=512, tm_in=256, l_pad=seq, hg_chunk=min(64, seq), hg_inner=max(1, min(8, seq // 64)),
                 hg_hps=1,
                 ssd_chunk=min(256, seq), tq=min(128, seq), tk=min(512, seq), hc=2)
    hg0 = jnp.zeros((bp, HG_HEADS, HG_KEY, HG_VAL), F32)
    ssm0 = jnp.zeros((bp, SSM_HEADS, SSM_STATE, SSM_HEAD_DIM), F32)
    conv0 = jnp.zeros((bp, SSM_CONV - 1, 2 * SSM_INNER), F32)
    pos_p = jnp.tile(_rope_tables(jnp.arange(seq, dtype=F32)), (bp, 1))
    y_p, hg_p, ssm_p, conv_p, ckv_p, kpe_p = _trunk(
        x_prompt.reshape(bp * seq, d), mod_all[:, :bp], pos_p, hg0, ssm0, conv0, None, p, w_lo, consts,
        bsz=bp, seq_len=seq, cdt=BF16, cfg=cfg_p)

    w_hi = _prep_weights(p, F32)
    pg = 32 if n_pages % 32 == 0 else n_pages
    cfg_s = dict(tm_ffn=128, fb=fb, tm_mm=128, tm_in=128, l_pad=SUBLANES, hg_chunk=SUBLANES, hg_inner=1, hg_hps=HG_HEADS,
                 ssd_chunk=SUBLANES, pg=pg)
    pos_s = _rope_tables(jnp.full((db,), past_len, F32))
    y_s, hg_s, ssm_s, conv_s, ckv_s, kpe_s = _trunk(
        x_sample.reshape(db, d), mod_all[:, bp:], pos_s, state_hgrn, state_ssm, state_conv,
        (cache_ckv, cache_kpe, page_table), p, w_hi, consts, bsz=db, seq_len=1, cdt=F32, cfg=cfg_s)

    return (y_p.reshape(bp, seq, d), y_s.reshape(db, 1, d), hg_p, hg_s, ssm_p, ssm_s, conv_p, conv_s,
            ckv_p.reshape(bp, seq, -1), ckv_s.reshape(db, 1, -1), kpe_p.reshape(bp, seq, -1),
            kpe_s.reshape(db, 1, -1))
```

```python
import functools
import math

import jax
import jax.numpy as jnp
import numpy as np
from jax import lax
from jax.experimental import pallas as pl
from jax.experimental.pallas import tpu as pltpu

F32 = jnp.float32
BF16 = jnp.bfloat16
HI = lax.Precision.HIGHEST

EPS = 1e-6
MACARON_W = 0.5
ROPE_THETA = 10000.0
PAGE_SIZE = 128

LANES = 128
SUBLANES = 8
VMEM_LIMIT = 56 * 1024 * 1024

HG_HEADS = 8
HG_KEY = 128
HG_VAL = 128
SSM_HEADS = 16
SSM_HEAD_DIM = 64
SSM_GROUPS = 4
SSM_STATE = 128
SSM_CONV = 4
SSM_INNER = SSM_HEADS * SSM_HEAD_DIM
SSM_GROUP_W = SSM_INNER // SSM_GROUPS
HEADS_PER_GROUP = SSM_HEADS // SSM_GROUPS
MLA_HEADS = 16
MLA_Q_RANK = 512
MLA_KV_RANK = 256
MLA_NOPE = 128
MLA_ROPE = 64
MLA_V = 128
MLA_SCALE = (MLA_NOPE + MLA_ROPE) ** -0.5
DECODE_SUB_PAGES = 8
FFN_AHEAD = 2
MLA_Q_AHEAD = 2
ATTN_AHEAD = 2
HG_SUB = SUBLANES


def _cparams(sem):
    return pltpu.CompilerParams(dimension_semantics=sem, vmem_limit_bytes=VMEM_LIMIT)


def _dot(a, b, cdt):
    if cdt == F32:
        return jnp.dot(a.astype(F32), b.astype(F32), precision=HI, preferred_element_type=F32)
    return jnp.dot(a.astype(BF16), b.astype(BF16), preferred_element_type=F32)


def _dot_nt(a, b, cdt):
    dn = (((1,), (1,)), ((), ()))
    if cdt == F32:
        return lax.dot_general(a.astype(F32), b.astype(F32), dn, precision=HI, preferred_element_type=F32)
    return lax.dot_general(a.astype(BF16), b.astype(BF16), dn, preferred_element_type=F32)


def _dot_hi(a, b):
    return jnp.dot(a, b, precision=HI, preferred_element_type=F32)


def _rms(x, g):
    r = lax.rsqrt(jnp.mean(x * x, axis=-1, keepdims=True) + EPS)
    return (x * r) * g


def _sigmoid(x):
    return 1.0 / (1.0 + jnp.exp(-x))


def _silu(x):
    return x * _sigmoid(x)


def _cumsum_rows(x):
    n = x.shape[0]
    row = lax.broadcasted_iota(jnp.int32, (n, 1), 0)
    s = 1
    while s < n:
        x = x + jnp.where(row >= s, pltpu.roll(x, s, 0), 0.0)
        s *= 2
    return x


def _expand_heads(a, e):
    hi = a.astype(BF16)
    r1 = a - hi.astype(F32)
    mid = r1.astype(BF16)
    lo = (r1 - mid.astype(F32)).astype(BF16)
    return (jnp.dot(hi, e, preferred_element_type=F32) + jnp.dot(mid, e, preferred_element_type=F32)
            + jnp.dot(lo, e, preferred_element_type=F32))


def _tile(n, pref):
    t = min(n, pref)
    while n % t:
        t -= SUBLANES
    return t


def _mod_spec(per_row, tm, tiles_per_seq, d):
    if per_row:
        return pl.BlockSpec((1, tm, d), lambda i, *_: (0, i, 0))
    return pl.BlockSpec((1, 1, d), lambda i, *_: (i // tiles_per_seq, 0, 0))


def _ada_kernel(c_ref, w_ref, b_ref, o_ref):
    cs = _silu(c_ref[...])
    o_ref[0] = _dot_hi(cs, w_ref[0]) + b_ref[0]


def _ada_mod(c_all, ada_w, ada_b):
    depth, d, n = ada_w.shape
    m = c_all.shape[0]
    tn = _tile(n, 1152)
    return pl.pallas_call(
        _ada_kernel,
        grid=(depth, n // tn),
        in_specs=[pl.BlockSpec((m, d), lambda l, j: (0, 0)),
                  pl.BlockSpec((1, d, tn), lambda l, j: (l, 0, j)),
                  pl.BlockSpec((1, 1, tn), lambda l, j: (l, 0, j))],
        out_specs=pl.BlockSpec((1, m, tn), lambda l, j: (l, 0, j)),
        out_shape=jax.ShapeDtypeStruct((depth, m, n), F32),
        compiler_params=_cparams(("parallel", "parallel")),
        name="ada_mod",
    )(c_all, ada_w, ada_b.reshape(depth, 1, n))


def _ffn_kernel(x_ref, gpre_ref, sh_ref, sc_ref, gt_ref, gpost_ref, wg_ref, wu_ref, wd_ref,
                o_ref, a_scr, *, cdt, fb):
    x = x_ref[...]
    h = (_rms(x, gpre_ref[...]) * (1.0 + sc_ref[0]) + sh_ref[0]).astype(cdt)
    nb = wg_ref.shape[1] // fb

    def gate_up(s):
        fs = slice(s * fb, (s + 1) * fb)
        return _dot(h, wg_ref[:, fs], cdt), _dot(h, wu_ref[:, fs], cdt)

    ahead = [gate_up(s) for s in range(min(FFN_AHEAD, nb))]
    for s in range(nb):
        g, u = ahead.pop(0)
        if s + FFN_AHEAD < nb:
            ahead.append(gate_up(s + FFN_AHEAD))
        a_scr[:, s * fb:(s + 1) * fb] = (_silu(g) * u).astype(cdt)
    y = _dot(a_scr[...], wd_ref[...], cdt)
    o_ref[...] = x + MACARON_W * gt_ref[0] * _rms(y, gpost_ref[...])


def _ffn(x, mods, gpre, gpost, wg, wu, wd, *, per_row, seq_len, cdt, tm_pref, fb):
    n, d = x.shape
    f = wg.shape[1]
    tm = _tile(seq_len if not per_row else n, tm_pref)
    tps = max(seq_len // tm, 1)
    shift, scale, gate = mods
    mspec = _mod_spec(per_row, tm, tps, d)
    vec = pl.BlockSpec((1, d), lambda i: (0, 0))
    whole = lambda a: pl.BlockSpec(a.shape, lambda i: (0, 0), pipeline_mode=pl.Buffered(1))
    return pl.pallas_call(
        functools.partial(_ffn_kernel, cdt=cdt, fb=fb),
        grid=(n // tm,),
        in_specs=[pl.BlockSpec((tm, d), lambda i: (i, 0)), vec, mspec, mspec, mspec, vec,
                  whole(wg), whole(wu), whole(wd)],
        out_specs=pl.BlockSpec((tm, d), lambda i: (i, 0)),
        out_shape=jax.ShapeDtypeStruct((n, d), F32),
        scratch_shapes=[pltpu.VMEM((tm, f), cdt)],
        compiler_params=_cparams(("parallel",)),
        name="ffn",
    )(x, gpre, shift, scale, gate, gpost, wg, wu, wd)


def _modmm_kernel(x_ref, gpre_ref, sh_ref, sc_ref, w_ref, wx_ref, o_ref, ox_ref, *, cdt, tn):
    h = (_rms(x_ref[...], gpre_ref[...]) * (1.0 + sc_ref[0]) + sh_ref[0]).astype(cdt)
    ox_ref[...] = _dot(h, wx_ref[...], cdt)
    for jb in range(w_ref.shape[1] // tn):
        cs = slice(jb * tn, (jb + 1) * tn)
        o_ref[:, cs] = _dot(h, w_ref[:, cs], cdt)


def _modmm(x, mods, gpre, w, wx, *, per_row, seq_len, cdt, tm_pref, tn):
    n, d = x.shape
    nout = w.shape[1]
    nx = wx.shape[1]
    tm = _tile(seq_len if not per_row else n, tm_pref)
    tps = max(seq_len // tm, 1)
    shift, scale, _ = mods
    mspec = _mod_spec(per_row, tm, tps, d)
    whole = lambda a: pl.BlockSpec(a.shape, lambda i: (0, 0), pipeline_mode=pl.Buffered(1))
    return pl.pallas_call(
        functools.partial(_modmm_kernel, cdt=cdt, tn=tn),
        grid=(n // tm,),
        in_specs=[pl.BlockSpec((tm, d), lambda i: (i, 0)),
                  pl.BlockSpec((1, d), lambda i: (0, 0)), mspec, mspec, whole(w), whole(wx)],
        out_specs=[pl.BlockSpec((tm, nout), lambda i: (i, 0)),
                   pl.BlockSpec((tm, nx), lambda i: (i, 0))],
        out_shape=[jax.ShapeDtypeStruct((n, nout), F32), jax.ShapeDtypeStruct((n, nx), F32)],
        compiler_params=_cparams(("parallel",)),
        name="modmm",
    )(x, gpre, shift, scale, w, wx)


def _hgrn_kernel(q_ref, f_ref, v_ref, g_ref, lb_ref, gn_ref, s0_ref, o_ref, sout_ref, st_scr,
                 *, chunk, n_inner, hps, l_true, nl, cdt):
    l = pl.program_id(2)
    t_blk = chunk * n_inner
    sub = HG_SUB
    nsub = chunk // sub

    @pl.when(l == 0)
    def _():
        for hh in range(hps):
            st_scr[hh] = s0_ref[0, hh].T

    rowc = lax.broadcasted_iota(jnp.int32, (chunk, 1), 0)
    tril = (lax.broadcasted_iota(jnp.int32, (chunk, chunk), 0)
            >= lax.broadcasted_iota(jnp.int32, (chunk, chunk), 1)).astype(F32)
    srow = lax.broadcasted_iota(jnp.int32, (nsub, sub, 1), 1)
    blocks = lambda a: a.reshape(nsub, sub, a.shape[-1])

    for hh in range(hps):
        cols = slice(hh * HG_KEY, (hh + 1) * HG_KEY)
        lbh = lb_ref[:, cols]
        gn = gn_ref[:, cols]
        pre = []
        for c in range(n_inner):
            rows = slice(c * chunk, (c + 1) * chunk)
            valid = (l * t_blk + c * chunk + rowc) < l_true
            fr = f_ref[rows, cols]
            v = v_ref[rows, cols]
            f = lbh + (1.0 - lbh) * _sigmoid(fr)
            logf = jnp.where(valid, jnp.log(f), 0.0)
            k = jnp.where(valid, (1.0 - lbh) * _sigmoid(-fr), 0.0)
            q = _silu(q_ref[rows, cols])
            b = _dot_hi(tril, logf)
            pre.append((rows, q, k, v, b))
        atts = []
        for rows, q, k, v, b in pre:
            att_c = []
            for i in range(1, nsub):
                lo = i * sub
                hi = min(chunk, -(-lo // 16) * 16)
                ref_b = b[lo - 1:lo, :]
                qi = q[lo:lo + sub, :] * jnp.exp(b[lo:lo + sub, :] - ref_b)
                ki = jnp.where(rowc[0:hi] < lo, k[0:hi, :] * jnp.exp(ref_b - b[0:hi, :]), 0.0)
                att_c.append((_dot_nt(qi, ki, cdt), hi))
            atts.append(att_c)
        intra, incs = [], []
        for (rows, q, k, v, b), att_c in zip(pre, atts):
            q3, k3, b3, v3 = blocks(q), blocks(k), blocks(b), blocks(v)
            o3 = jnp.sum(q3 * k3, axis=-1, keepdims=True) * v3
            for d in range(1, sub):
                k_sh = pltpu.roll(k3, d, 1)
                b_sh = pltpu.roll(b3, d, 1)
                v_sh = pltpu.roll(v3, d, 1)
                a = jnp.sum(q3 * k_sh * jnp.exp(b3 - b_sh), axis=-1, keepdims=True)
                o3 = o3 + jnp.where(srow >= d, a, 0.0) * v_sh
            o = o3.reshape(chunk, HG_VAL)
            if nsub > 1:
                parts = [jnp.zeros((sub, HG_VAL), F32)] + [_dot(att, v[0:hi, :], cdt) for att, hi in att_c]
                o = o + jnp.concatenate(parts, axis=0)
            intra.append(o)
            bl = b[chunk - 1:chunk, :]
            incs.append(_dot(v.T, k * jnp.exp(bl - b), cdt))
        st = st_scr[hh]
        for (rows, q, k, v, b), o, inc in zip(pre, intra, incs):
            o = o + _dot_nt(q * jnp.exp(b), st, cdt)
            st = st * jnp.exp(b[chunk - 1:chunk, :]) + inc
            o_ref[rows, cols] = _rms(o, gn) * _silu(g_ref[rows, cols])
        st_scr[hh] = st

    @pl.when(l == nl - 1)
    def _():
        for hh in range(hps):
            sout_ref[0, hh] = st_scr[hh].T


def _hgrn(proj, lb, gn, s0, *, bsz, l_pad, l_true, chunk, n_inner, hps, cdt):
    t_blk = chunk * n_inner
    nl = l_pad // t_blk
    n = bsz * l_pad
    nhb = HG_HEADS // hps
    wide = hps * HG_KEY

    def col(cb):
        return pl.BlockSpec((t_blk, wide), lambda b, h, l: (b * nl + l, cb * nhb + h))

    hvec = pl.BlockSpec((1, wide), lambda b, h, l: (0, h))
    sspec = pl.BlockSpec((1, hps, HG_KEY, HG_VAL), lambda b, h, l: (b, h, 0, 0))
    return pl.pallas_call(
        functools.partial(_hgrn_kernel, chunk=chunk, n_inner=n_inner, hps=hps, l_true=l_true, nl=nl, cdt=cdt),
        grid=(bsz, nhb, nl),
        in_specs=[col(0), col(1), col(2), col(3), hvec, hvec, sspec],
        out_specs=[pl.BlockSpec((t_blk, wide), lambda b, h, l: (b * nl + l, h)), sspec],
        out_shape=[jax.ShapeDtypeStruct((n, HG_HEADS * HG_VAL), F32),
                   jax.ShapeDtypeStruct(s0.shape, F32)],
        scratch_shapes=[pltpu.VMEM((hps, HG_VAL, HG_KEY), F32)],
        compiler_params=_cparams(("parallel", "parallel", "arbitrary")),
        name="hgrn2",
    )(proj, proj, proj, proj, lb, gn, s0)


def _hgrn_step_kernel(q_ref, f_ref, v_ref, g_ref, lb_ref, gn_ref, s0_ref, o_ref, sout_ref):
    nr = q_ref.shape[0]
    for h in range(HG_HEADS):
        cols = slice(h * HG_KEY, (h + 1) * HG_KEY)
        lbh = lb_ref[:, cols]
        fr = f_ref[:, cols]
        f_t = (lbh + (1.0 - lbh) * _sigmoid(fr)).T
        k_t = ((1.0 - lbh) * _sigmoid(-fr)).T
        q_t = _silu(q_ref[:, cols]).T
        v = v_ref[:, cols]
        rows = []
        for r in range(nr):
            s_new = f_t[:, r:r + 1] * s0_ref[r, h] + k_t[:, r:r + 1] * v[r:r + 1, :]
            sout_ref[r, h] = s_new
            rows.append(jnp.sum(q_t[:, r:r + 1] * s_new, axis=0, keepdims=True))
        o = jnp.concatenate(rows, axis=0)
        o_ref[:, cols] = _rms(o, gn_ref[:, cols]) * _silu(g_ref[:, cols])


def _hgrn_step(proj, lb, gn, s0, *, nr):
    n = proj.shape[0]
    wide = HG_HEADS * HG_KEY
    col = lambda cb: pl.BlockSpec((nr, wide), lambda i: (i, cb))
    vecs = pl.BlockSpec((1, wide), lambda i: (0, 0))
    sspec = pl.BlockSpec((nr, HG_HEADS, HG_KEY, HG_VAL), lambda i: (i, 0, 0, 0))
    return pl.pallas_call(
        _hgrn_step_kernel,
        grid=(n // nr,),
        in_specs=[col(0), col(1), col(2), col(3), vecs, vecs, sspec],
        out_specs=[pl.BlockSpec((nr, wide), lambda i: (i, 0)), sspec],
        out_shape=[jax.ShapeDtypeStruct((n, wide), F32), jax.ShapeDtypeStruct(s0.shape, F32)],
        compiler_params=_cparams(("parallel",)),
        name="hgrn2_step",
    )(proj, proj, proj, proj, lb, gn, s0)


def _ssd_step_kernel(z_ref, xa_ref, xb_ref, dt_ref, cw_ref, cb_ref, dtb_ref, alog_ref, dexp_ref, nrm_ref,
                     e_ref, cprev_ref, s0_ref, y_ref, sout_ref):
    nr = z_ref.shape[0]
    gw = SSM_GROUP_W
    xraw = jnp.concatenate([xa_ref[...], xb_ref[...]], axis=-1)
    conv = cb_ref[...] + cw_ref[SSM_CONV - 1:SSM_CONV, :] * xraw
    for i in range(SSM_CONV - 1):
        conv = conv + cw_ref[i:i + 1, :] * cprev_ref[:, i, :]
    xc = _silu(conv)
    xh = xc[:, 0:SSM_INNER]
    bm = xc[:, SSM_INNER:SSM_INNER + SSM_GROUPS * SSM_STATE]
    cm = xc[:, SSM_INNER + SSM_GROUPS * SSM_STATE:]
    draw = dt_ref[...] + dtb_ref[...]
    dt = jnp.maximum(draw, 0.0) + jnp.log1p(jnp.exp(-jnp.abs(draw)))
    e = e_ref[...]
    dec_t = _expand_heads(jnp.exp(dt * (-jnp.exp(alog_ref[...]))), e).T
    xd_t = (xh * _expand_heads(dt, e)).T
    for h in range(SSM_HEADS):
        g = h // HEADS_PER_GROUP
        hs = slice(h * SSM_HEAD_DIM, (h + 1) * SSM_HEAD_DIM)
        bg = bm[:, g * SSM_STATE:(g + 1) * SSM_STATE]
        for r in range(nr):
            sout_ref[r, h] = dec_t[hs, r:r + 1] * s0_ref[r, h] + xd_t[hs, r:r + 1] * bg[r:r + 1, :]
    y_parts = []
    for h in range(SSM_HEADS):
        g = h // HEADS_PER_GROUP
        cg = cm[:, g * SSM_STATE:(g + 1) * SSM_STATE]
        cols = [jnp.sum(sout_ref[r, h] * cg[r:r + 1, :], axis=-1, keepdims=True) for r in range(nr)]
        y_parts.append(jnp.concatenate(cols, axis=-1))
    y = jnp.concatenate(y_parts, axis=0).T
    y = (y + dexp_ref[...] * xh) * _silu(z_ref[...])
    outs = []
    for g in range(SSM_GROUPS):
        gs = slice(g * gw, (g + 1) * gw)
        outs.append(_rms(y[:, gs], nrm_ref[:, gs]))
    y_ref[...] = jnp.concatenate(outs, axis=-1)


def _ssd_step(proj, dtraw, cw, cb, dtb, alog, dexp, nrm, emat, cprev, s0t, *, nr):
    n = proj.shape[0]
    w = SSM_INNER
    zcol = 4 * HG_HEADS * HG_KEY // w
    cc = 2 * w
    col = lambda cbk: pl.BlockSpec((nr, w), lambda i: (i, cbk))
    const = lambda shape: pl.BlockSpec(shape, lambda i: (0,) * len(shape))
    sspec = pl.BlockSpec((nr, SSM_HEADS, SSM_HEAD_DIM, SSM_STATE), lambda i: (i, 0, 0, 0))
    return pl.pallas_call(
        _ssd_step_kernel,
        grid=(n // nr,),
        in_specs=[col(zcol), col(zcol + 1), col(zcol + 2), pl.BlockSpec((nr, LANES), lambda i: (i, 0)),
                  const((SSM_CONV, cc)), const((1, cc)), const((1, LANES)), const((1, LANES)),
                  const((1, w)), const((1, w)), const((LANES, w)),
                  pl.BlockSpec((nr, SSM_CONV - 1, cc), lambda i: (i, 0, 0)), sspec],
        out_specs=[pl.BlockSpec((nr, w), lambda i: (i, 0)), sspec],
        out_shape=[jax.ShapeDtypeStruct((n, w), F32), jax.ShapeDtypeStruct(s0t.shape, F32)],
        compiler_params=_cparams(("parallel",)),
        name="ssd_step",
    )(proj, proj, proj, dtraw, cw, cb, dtb, alog, dexp, nrm, emat, cprev, s0t)


def _ssd_kernel(z_ref, xa_ref, xb_ref, dt_ref, cw_ref, cb_ref, dtb_ref, alog_ref, dexp_ref, nrm_ref,
                e_ref, cprev_ref, s0_ref, y_ref, sout_ref, xpad_scr, s_scr, *, chunk, l_true, nl, cdt):
    l = pl.program_id(1)
    c = chunk
    gw = SSM_GROUP_W

    @pl.when(l == 0)
    def _():
        xpad_scr[0:SUBLANES, :] = cprev_ref[0]
        for g in range(SSM_GROUPS):
            s_scr[g] = s0_ref[0, g * HEADS_PER_GROUP:(g + 1) * HEADS_PER_GROUP].reshape(gw, SSM_STATE)

    @pl.when(l > 0)
    def _():
        xpad_scr[0:SUBLANES, :] = xpad_scr[c:c + SUBLANES, :]

    xpad_scr[SUBLANES:SUBLANES + c, 0:SSM_INNER] = xa_ref[...]
    xpad_scr[SUBLANES:SUBLANES + c, SSM_INNER:2 * SSM_INNER] = xb_ref[...]
    conv = cb_ref[...]
    for i in range(SSM_CONV):
        lo = SUBLANES - (SSM_CONV - 1) + i
        conv = conv + cw_ref[i:i + 1, :] * xpad_scr[lo:lo + c, :]
    xc = _silu(conv)
    xh = xc[:, 0:SSM_INNER]
    bm = xc[:, SSM_INNER:SSM_INNER + SSM_GROUPS * SSM_STATE]
    cm = xc[:, SSM_INNER + SSM_GROUPS * SSM_STATE:]

    rowc = lax.broadcasted_iota(jnp.int32, (c, 1), 0)
    valid = (l * c + rowc) < l_true
    draw = dt_ref[...] + dtb_ref[...]
    dt = jnp.maximum(draw, 0.0) + jnp.log1p(jnp.exp(-jnp.abs(draw)))
    dt = jnp.where(valid, dt, 0.0)
    la = dt * (-jnp.exp(alog_ref[...]))
    ri = lax.broadcasted_iota(jnp.int32, (c, c), 0)
    ci = lax.broadcasted_iota(jnp.int32, (c, c), 1)
    causal = ri >= ci
    b = _cumsum_rows(la)
    b_t = b.T
    dt_t = dt.T
    bl = b[c - 1:c, :]
    expand = functools.partial(_expand_heads, e=e_ref[...])

    lane = lax.broadcasted_iota(jnp.int32, (1, gw), 1)
    grp = lambda a, g: a[:, g * SSM_STATE:(g + 1) * SSM_STATE]
    gsl = lambda g: slice(g * gw, (g + 1) * gw)
    cbs = [_dot_nt(grp(cm, g), grp(bm, g), cdt) for g in range(SSM_GROUPS)]
    carried = [_dot_nt(grp(cm, g), s_scr[g], cdt) for g in range(SSM_GROUPS)]
    eb_e = expand(jnp.exp(b))
    dtw_e = expand(dt * jnp.exp(bl - b))
    for g in range(SSM_GROUPS):
        gs = gsl(g)
        decay = jnp.broadcast_to(eb_e[c - 1:c, gs], (SUBLANES, gw)).T[:, 0:1]
        s_scr[g] = decay * s_scr[g] + _dot((xh[:, gs] * dtw_e[:, gs]).T, grp(bm, g), cdt)
    ys = []
    for g in range(SSM_GROUPS):
        gs = gsl(g)
        yg = carried[g] * eb_e[:, gs]
        for j in range(HEADS_PER_GROUP):
            h = g * HEADS_PER_GROUP + j
            seg = jnp.exp(jnp.where(causal, b[:, h:h + 1] - b_t[h:h + 1, :], -jnp.inf)) * dt_t[h:h + 1, :]
            in_head = (lane >= j * SSM_HEAD_DIM) & (lane < (j + 1) * SSM_HEAD_DIM)
            yg = yg + _dot(cbs[g] * seg, jnp.where(in_head, xh[:, gs], 0.0), cdt)
        ys.append(yg)
    y = jnp.concatenate(ys, axis=-1)
    y = (y + dexp_ref[...] * xh) * _silu(z_ref[...])
    outs = []
    for g in range(SSM_GROUPS):
        gs = slice(g * gw, (g + 1) * gw)
        outs.append(_rms(y[:, gs], nrm_ref[:, gs]))
    y_ref[...] = jnp.concatenate(outs, axis=-1)

    @pl.when(l == nl - 1)
    def _():
        for g in range(SSM_GROUPS):
            sout_ref[0, g * HEADS_PER_GROUP:(g + 1) * HEADS_PER_GROUP] = s_scr[g].reshape(
                HEADS_PER_GROUP, SSM_HEAD_DIM, SSM_STATE)


def _ssd(proj, dtraw, cw, cb, dtb, alog, dexp, nrm, emat, cprev, s0g, *, bsz, l_pad, l_true, chunk, cdt):
    nl = l_pad // chunk
    n = bsz * l_pad
    w = SSM_INNER
    zcol = 4 * HG_HEADS * HG_KEY // w
    cc = 2 * w

    def col(cbk):
        return pl.BlockSpec((chunk, w), lambda b, l: (b * nl + l, cbk))

    def const(shape):
        return pl.BlockSpec(shape, lambda b, l: (0,) * len(shape))

    sspec = pl.BlockSpec((1, SSM_HEADS, SSM_HEAD_DIM, SSM_STATE), lambda b, l: (b, 0, 0, 0))
    return pl.pallas_call(
        functools.partial(_ssd_kernel, chunk=chunk, l_true=l_true, nl=nl, cdt=cdt),
        grid=(bsz, nl),
        in_specs=[col(zcol), col(zcol + 1), col(zcol + 2),
                  pl.BlockSpec((chunk, LANES), lambda b, l: (b * nl + l, 0)),
                  const((SSM_CONV, cc)), const((1, cc)), const((1, LANES)), const((1, LANES)),
                  const((1, w)), const((1, w)), const((LANES, w)),
                  pl.BlockSpec((1, SUBLANES, cc), lambda b, l: (b, 0, 0)), sspec],
        out_specs=[pl.BlockSpec((chunk, w), lambda b, l: (b * nl + l, 0)), sspec],
        out_shape=[jax.ShapeDtypeStruct((n, w), F32), jax.ShapeDtypeStruct(s0g.shape, F32)],
        scratch_shapes=[pltpu.VMEM((chunk + 2 * SUBLANES, cc), F32),
                        pltpu.VMEM((SSM_GROUPS, SSM_GROUP_W, SSM_STATE), F32)],
        compiler_params=_cparams(("parallel", "arbitrary")),
        name="ssd",
    )(proj, proj, proj, dtraw, cw, cb, dtb, alog, dexp, nrm, emat, cprev, s0g)


def _outproj_kernel(x_ref, a_ref, b_ref, w_ref, gt_ref, gpost_ref, o_ref, *, cdt):
    ka = a_ref.shape[1]
    y = _dot(a_ref[...], w_ref[0:ka, :], cdt) + _dot(b_ref[...], w_ref[ka:, :], cdt)
    o_ref[...] = x_ref[...] + gt_ref[0] * _rms(y, gpost_ref[...])


def _outproj(x, a, b, w, gate, gpost, *, per_row, seq_len, cdt, tm_pref):
    n, d = x.shape
    tm = _tile(seq_len if not per_row else n, tm_pref)
    tps = max(seq_len // tm, 1)
    mspec = _mod_spec(per_row, tm, tps, d)
    row = lambda width: pl.BlockSpec((tm, width), lambda i: (i, 0))
    return pl.pallas_call(
        functools.partial(_outproj_kernel, cdt=cdt),
        grid=(n // tm,),
        in_specs=[row(d), row(a.shape[1]), row(b.shape[1]),
                  pl.BlockSpec(w.shape, lambda i: (0, 0)), mspec,
                  pl.BlockSpec((1, d), lambda i: (0, 0))],
        out_specs=row(d),
        out_shape=jax.ShapeDtypeStruct((n, d), F32),
        compiler_params=_cparams(("parallel",)),
        name="outproj",
    )(x, a, b, w, gate, gpost)


def _rope(xs, cs_t):
    u = xs * cs_t
    return u + pltpu.roll(u, MLA_ROPE, 1)


def _mlaq_kernel(qr_ref, kv_ref, gq_ref, gkv_ref, wuq_ref, wuk_ref, cs_ref,
                 qx_ref, ckv_ref, kpe_ref, kx_ref, kxt_ref, *, cdt, q_scale):
    cs_t = cs_ref[...]
    lane = lax.broadcasted_iota(jnp.int32, (1, 2 * MLA_ROPE), 1)
    low = lane < MLA_ROPE
    cq = _rms(qr_ref[...], gq_ref[...]).astype(cdt)
    kv = kv_ref[...]
    ckv = _rms(kv[:, 0:MLA_KV_RANK], gkv_ref[...])
    kpe = _rope(kv[:, MLA_KV_RANK:MLA_KV_RANK + 2 * MLA_ROPE], cs_t)
    ckv_ref[...] = ckv
    kpe_ref[...] = kpe[:, 0:MLA_ROPE]
    kx = jnp.concatenate([ckv, jnp.where(low, kpe, 0.0)], axis=-1)
    kx_ref[...] = kx.astype(kx_ref.dtype)
    kxt_ref[...] = kx.T.astype(kxt_ref.dtype)
    up = lambda h: _dot(cq, wuq_ref[h], cdt)
    ahead = [up(h) for h in range(MLA_Q_AHEAD)]
    for h in range(MLA_HEADS):
        q = ahead.pop(0)
        if h + MLA_Q_AHEAD < MLA_HEADS:
            ahead.append(up(h + MLA_Q_AHEAD))
        ql = _dot(q[:, 0:MLA_NOPE], wuk_ref[h], cdt)
        qp = _rope(q[:, MLA_NOPE:MLA_NOPE + 2 * MLA_ROPE], cs_t)
        qx_ref[h, :, 0:MLA_KV_RANK] = (ql * q_scale).astype(qx_ref.dtype)
        qx_ref[h, :, MLA_KV_RANK:] = jnp.where(low, qp * q_scale, 0.0).astype(qx_ref.dtype)


def _mlaq(qraw, kvraw, gq, gkv, wuq, wuk, cs_t, *, cdt, tm_pref, q_scale):
    n = qraw.shape[0]
    tm = _tile(n, tm_pref)
    h = MLA_HEADS
    kw = MLA_KV_RANK + 2 * MLA_ROPE
    row = lambda width: pl.BlockSpec((tm, width), lambda i: (i, 0))
    const = lambda shape: pl.BlockSpec(shape, lambda i: (0,) * len(shape))
    return pl.pallas_call(
        functools.partial(_mlaq_kernel, cdt=cdt, q_scale=q_scale),
        grid=(n // tm,),
        in_specs=[row(MLA_Q_RANK), row(kvraw.shape[1]), const((1, MLA_Q_RANK)), const((1, MLA_KV_RANK)),
                  const(wuq.shape), const(wuk.shape), row(2 * MLA_ROPE)],
        out_specs=[pl.BlockSpec((h, tm, kw), lambda i: (0, i, 0)), row(MLA_KV_RANK), row(MLA_ROPE), row(kw),
                   pl.BlockSpec((kw, tm), lambda i: (0, i))],
        out_shape=[jax.ShapeDtypeStruct((h, n, kw), cdt),
                   jax.ShapeDtypeStruct((n, MLA_KV_RANK), F32),
                   jax.ShapeDtypeStruct((n, MLA_ROPE), F32),
                   jax.ShapeDtypeStruct((n, kw), BF16),
                   jax.ShapeDtypeStruct((kw, n), BF16)],
        compiler_params=_cparams(("parallel",)),
        name="mla_q",
    )(qraw, kvraw, gq, gkv, wuq, wuk, cs_t)


def _attn_kernel(it_ref, jt_ref, qx_ref, kx_ref, kxt_ref, o_ref, m_scr, l_scr, acc_scr, *, tq, tk, hc):
    step = pl.program_id(1)
    i = it_ref[step]
    j = jt_ref[step]
    last = (i * tq + tq - 1) // tk

    @pl.when(j == 0)
    def _():
        m_scr[...] = jnp.full_like(m_scr, -jnp.inf)
        l_scr[...] = jnp.zeros_like(l_scr)
        acc_scr[...] = jnp.zeros_like(acc_scr)

    rc = hc * tq
    kw = qx_ref.shape[2]

    def key_tile(masked, width):
        kc = kx_ref[0:width, 0:MLA_KV_RANK]
        if masked:
            qpos = i * tq + (lax.broadcasted_iota(jnp.int32, (rc, width), 0) & (tq - 1))
            kpos = j * tk + lax.broadcasted_iota(jnp.int32, (rc, width), 1)
            keep = kpos <= qpos
        logits = lambda c: _dot(qx_ref[c * hc:(c + 1) * hc].reshape(rc, kw), kxt_ref[:, 0:width], BF16)
        nblk = MLA_HEADS // hc
        ahead = [logits(c) for c in range(min(ATTN_AHEAD, nblk))]
        for c in range(nblk):
            rs = slice(c * rc, (c + 1) * rc)
            s = ahead.pop(0)
            if c + ATTN_AHEAD < nblk:
                ahead.append(logits(c + ATTN_AHEAD))
            if masked:
                s = jnp.where(keep, s, -jnp.inf)
            m_prev = m_scr[rs]
            m_new = jnp.maximum(m_prev, jnp.max(s, axis=-1, keepdims=True))
            alpha = jnp.exp2(m_prev - m_new)
            p = jnp.exp2(s - m_new)
            l_scr[rs] = alpha * l_scr[rs] + jnp.sum(p, axis=-1, keepdims=True)
            acc_scr[rs] = alpha * acc_scr[rs] + _dot(p, kc, BF16)
            m_scr[rs] = m_new

    @pl.when(j < last)
    def _():
        key_tile(False, tk)

    @pl.when(j == last)
    def _():
        for v in range(tk // tq):
            @pl.when(i * tq - j * tk == v * tq)
            def _():
                key_tile(True, (v + 1) * tq)
        o = acc_scr[...] / l_scr[...]
        o_ref[...] = o.reshape(MLA_HEADS, tq, MLA_KV_RANK).astype(o_ref.dtype)


def _attn(qx, kx, kxt, *, bsz, seq_len, tq, tk, hc):
    h, n, kw = qx.shape
    r = MLA_KV_RANK
    nq = seq_len // tq
    nk = seq_len // tk
    rows = h * tq
    pairs = [(i, j) for i in range(nq) for j in range((i * tq + tq - 1) // tk + 1)]
    i_of = jnp.asarray([ij[0] for ij in pairs], jnp.int32)
    j_of = jnp.asarray([ij[1] for ij in pairs], jnp.int32)
    grid_spec = pltpu.PrefetchScalarGridSpec(
        num_scalar_prefetch=2,
        grid=(bsz, len(pairs)),
        in_specs=[pl.BlockSpec((h, tq, kw), lambda b, s, it, jt: (0, b * nq + it[s], 0)),
                  pl.BlockSpec((tk, kw), lambda b, s, it, jt: (b * nk + jt[s], 0)),
                  pl.BlockSpec((kw, tk), lambda b, s, it, jt: (0, b * nk + jt[s]))],
        out_specs=pl.BlockSpec((h, tq, r), lambda b, s, it, jt: (0, b * nq + it[s], 0)),
        scratch_shapes=[pltpu.VMEM((rows, 1), F32), pltpu.VMEM((rows, 1), F32), pltpu.VMEM((rows, r), F32)],
    )
    return pl.pallas_call(
        functools.partial(_attn_kernel, tq=tq, tk=tk, hc=hc),
        grid_spec=grid_spec,
        out_shape=jax.ShapeDtypeStruct((h, n, r), BF16),
        compiler_params=_cparams(("parallel", "arbitrary")),
        name="mla_attn",
    )(i_of, j_of, qx, kx, kxt)


def _decode_kernel(pt_ref, qx_ref, ckn_ref, kpn_ref, cache_c, cache_p, o_ref, cbuf, pbuf, sems,
                   *, n_pages, pg, sub_pg, n_seq):
    bidx = pl.program_id(0)
    n_chunks = n_pages // pg
    total_chunks = n_seq * n_chunks
    sub = sub_pg * PAGE_SIZE

    def page_copies(page, slot, p):
        return (pltpu.make_async_copy(cache_c.at[page], cbuf.at[slot, p], sems.at[0, slot]),
                pltpu.make_async_copy(cache_p.at[page], pbuf.at[slot, p], sems.at[1, slot]))

    def start(g, slot):
        for p in range(pg):
            for cp in page_copies(pt_ref[g * pg + p], slot, p):
                cp.start()

    def wait(slot):
        for p in range(pg):
            for cp in page_copies(0, slot, p):
                cp.wait()

    qx = qx_ref[0]
    ql = qx[:, 0:MLA_KV_RANK]
    qp = qx[:, MLA_KV_RANK:MLA_KV_RANK + MLA_ROPE]

    @pl.when(bidx == 0)
    def _():
        start(0, 0)

    def body(cidx, carry):
        m_prev, l_prev, acc = carry
        g = bidx * n_chunks + cidx
        slot = g % 2

        start(jnp.minimum(g + 1, total_chunks - 1), 1 - slot)
        wait(slot)
        kcs, s_parts = [], []
        for sb in range(pg // sub_pg):
            kc = cbuf[slot, sb * sub_pg:(sb + 1) * sub_pg].reshape(sub, MLA_KV_RANK).astype(BF16)
            kcs.append(kc)
            s_rope = jnp.concatenate(
                [_dot(qp, pbuf[slot, sb * sub_pg + p], BF16) for p in range(sub_pg)], axis=-1)
            s_parts.append(_dot_nt(ql, kc, BF16) + s_rope)
        s = jnp.concatenate(s_parts, axis=-1) * MLA_SCALE
        m_new = jnp.maximum(m_prev, jnp.max(s, axis=-1, keepdims=True))
        alpha = jnp.exp(m_prev - m_new)
        p = jnp.exp(s - m_new)
        l_new = alpha * l_prev + jnp.sum(p, axis=-1, keepdims=True)
        pv = _dot(p[:, 0:sub], kcs[0], BF16)
        for sb in range(1, pg // sub_pg):
            pv = pv + _dot(p[:, sb * sub:(sb + 1) * sub], kcs[sb], BF16)
        return m_new, l_new, alpha * acc + pv

    init = (jnp.full((MLA_HEADS, 1), -jnp.inf, F32), jnp.zeros((MLA_HEADS, 1), F32),
            jnp.zeros((MLA_HEADS, MLA_KV_RANK), F32))
    m_prev, l_prev, acc = lax.fori_loop(0, n_chunks, body, init)

    @pl.when(bidx == n_seq - 1)
    def _():
        wait(total_chunks % 2)

    ckn = ckn_ref[0]
    s_own = (jnp.sum(ql * ckn, axis=-1, keepdims=True)
             + jnp.sum(qp * kpn_ref[0], axis=-1, keepdims=True)) * MLA_SCALE
    m_new = jnp.maximum(m_prev, s_own)
    alpha = jnp.exp(m_prev - m_new)
    p_own = jnp.exp(s_own - m_new)
    l_new = alpha * l_prev + p_own
    o_ref[0] = (alpha * acc + p_own * ckn) / l_new


def _decode(page_table, qx, ckn, kpn, cache_c, cache_pt, *, pg):
    db, n_pages = page_table.shape
    h, r, dr = MLA_HEADS, MLA_KV_RANK, MLA_ROPE
    grid_spec = pltpu.PrefetchScalarGridSpec(
        num_scalar_prefetch=1,
        grid=(db,),
        in_specs=[pl.BlockSpec((1, h, qx.shape[2]), lambda b, pt: (b, 0, 0)),
                  pl.BlockSpec((1, 1, r), lambda b, pt: (b, 0, 0)),
                  pl.BlockSpec((1, 1, dr), lambda b, pt: (b, 0, 0)),
                  pl.BlockSpec(memory_space=pl.ANY),
                  pl.BlockSpec(memory_space=pl.ANY)],
        out_specs=pl.BlockSpec((1, h, r), lambda b, pt: (b, 0, 0)),
        scratch_shapes=[pltpu.VMEM((2, pg, PAGE_SIZE, r), F32),
                        pltpu.VMEM((2, pg, dr, PAGE_SIZE), F32),
                        pltpu.SemaphoreType.DMA((2, 2))],
    )
    return pl.pallas_call(
        functools.partial(_decode_kernel, n_pages=n_pages, pg=pg, sub_pg=min(DECODE_SUB_PAGES, pg), n_seq=db),
        grid_spec=grid_spec,
        out_shape=jax.ShapeDtypeStruct((db, h, r), F32),
        compiler_params=_cparams(("arbitrary",)),
        name="mla_decode",
    )(page_table.reshape(-1), qx, ckn, kpn, cache_c, cache_pt)


def _mlaout_kernel(x_ref, ol_ref, wuv_ref, wo_ref, gt_ref, gpost_ref, o_ref, *, cdt):
    t = jnp.concatenate([_dot(ol_ref[h], wuv_ref[h], cdt).astype(cdt) for h in range(MLA_HEADS)], axis=-1)
    y = _dot(t, wo_ref[...], cdt)
    o_ref[...] = x_ref[...] + gt_ref[0] * _rms(y, gpost_ref[...])


def _mlaout(x, ol, wuv, wo, gate, gpost, *, per_row, seq_len, cdt, tm_pref):
    n, d = x.shape
    tm = _tile(seq_len if not per_row else n, tm_pref)
    tps = max(seq_len // tm, 1)
    mspec = _mod_spec(per_row, tm, tps, d)
    return pl.pallas_call(
        functools.partial(_mlaout_kernel, cdt=cdt),
        grid=(n // tm,),
        in_specs=[pl.BlockSpec((tm, d), lambda i: (i, 0)),
                  pl.BlockSpec((MLA_HEADS, tm, MLA_KV_RANK), lambda i: (0, i, 0)),
                  pl.BlockSpec(wuv.shape, lambda i: (0, 0, 0)),
                  pl.BlockSpec(wo.shape, lambda i: (0, 0)),
                  mspec, pl.BlockSpec((1, d), lambda i: (0, 0))],
        out_specs=pl.BlockSpec((tm, d), lambda i: (i, 0)),
        out_shape=jax.ShapeDtypeStruct((n, d), F32),
        compiler_params=_cparams(("parallel",)),
        name="mla_out",
    )(x, ol, wuv, wo, gate, gpost)


def _prep_weights(p, cdt):
    d = p['w_in_ab'].shape[0]
    n_main = 4 * HG_HEADS * HG_KEY + SSM_INNER + 2 * SSM_INNER
    w_in = p['w_in_ab']
    w_dt = jnp.pad(w_in[:, n_main:], ((0, 0), (0, LANES - SSM_HEADS)))
    exch = lambda a: jnp.concatenate([a, jnp.roll(a[..., -MLA_ROPE:], MLA_ROPE // 2, axis=-1)], axis=-1)
    w_dkv = exch(p['mla_w_dkv'])
    w_uq = exch(p['mla_w_uq'].reshape(MLA_Q_RANK, MLA_HEADS, MLA_NOPE + MLA_ROPE)).transpose(1, 0, 2)
    c = lambda a: a.astype(cdt)
    return dict(
        ffn_wg=c(p['ffn_wg']), ffn_wu=c(p['ffn_wu']), ffn_wd=c(p['ffn_wd']),
        w_in=c(w_in[:, :n_main]), w_dt=c(w_dt), w_out=c(p['w_out_ab']),
        w_dq=c(p['mla_w_dq']), w_dkv=c(w_dkv),
        w_uq=c(w_uq),
        w_uk=c(p['mla_w_uk'].transpose(1, 2, 0)),
        w_uv=c(p['mla_w_uv'].transpose(1, 0, 2)),
        w_o=c(p['mla_w_o']),
    )


def _rope_tables(pos):
    half = MLA_ROPE // 2
    inv = ROPE_THETA ** (-jnp.arange(half, dtype=F32) / half)
    ang = pos[:, None] * inv[None]
    cos, sin = jnp.cos(ang), jnp.sin(ang)
    return jnp.concatenate([cos, cos, -sin, sin], axis=-1)


def _trunk(x, mod_all, pos, hg_s0, ssm_s0, conv_prev, past, p, w, consts, *, bsz, seq_len, cdt, cfg):
    n, d = x.shape
    per_row = seq_len == 1
    vec = lambda a: a.reshape(1, -1)

    def mods(layer, sub):
        m = mod_all[layer].reshape(bsz, 3, 3, d)[:, sub]
        if per_row:
            return tuple(m[:, k].reshape(1, bsz, d) for k in range(3))
        return tuple(m[:, k].reshape(bsz, 1, d) for k in range(3))

    common = dict(per_row=per_row, seq_len=seq_len, cdt=cdt)

    def ffn(x, layer, which, sub):
        return _ffn(x, mods(layer, sub), vec(p['norm_pre'][layer, sub]), vec(p['norm_post'][layer, sub]),
                    w['ffn_wg'][layer, which], w['ffn_wu'][layer, which], w['ffn_wd'][layer, which],
                    tm_pref=cfg['tm_ffn'], fb=cfg['fb'], **common)

    x = ffn(x, 0, 0, 0)
    m1 = mods(0, 1)
    proj, dtraw = _modmm(x, m1, vec(p['norm_pre'][0, 1]), w['w_in'], w['w_dt'],
                         tm_pref=cfg['tm_in'], tn=SSM_INNER, **common)
    lb = jnp.cumsum(jax.nn.softmax(p['hg_lb_logits'].astype(F32), axis=0), axis=0)[0]
    pad16 = lambda a: jnp.pad(a.astype(F32), (0, LANES - SSM_HEADS)).reshape(1, LANES)
    ssm_vecs = (p['ssm_conv_w'], vec(p['ssm_conv_b']), pad16(p['ssm_dt_bias']), pad16(p['ssm_a_log']),
                vec(jnp.repeat(p['ssm_d'].astype(F32), SSM_HEAD_DIM)), vec(p['ssm_norm']))
    ssm_s0t = ssm_s0.transpose(0, 1, 3, 2)
    if per_row:
        nr = _tile(bsz, SUBLANES)
        o_a, hg_s = _hgrn_step(proj, vec(lb), vec(p['hg_norm']), hg_s0, nr=nr)
        y_b, ssm_st = _ssd_step(proj, dtraw, *ssm_vecs, consts['emat'], conv_prev, ssm_s0t, nr=nr)
    else:
        o_a, hg_s = _hgrn(proj, vec(lb), vec(p['hg_norm']), hg_s0, bsz=bsz, l_pad=seq_len, l_true=seq_len,
                          chunk=cfg['hg_chunk'], n_inner=cfg['hg_inner'], hps=cfg['hg_hps'], cdt=cdt)
        cprev8 = jnp.pad(conv_prev, ((0, 0), (SUBLANES - (SSM_CONV - 1), 0), (0, 0)))
        y_b, ssm_st = _ssd(proj, dtraw, *ssm_vecs, consts['emat'], cprev8, ssm_s0t,
                           bsz=bsz, l_pad=seq_len, l_true=seq_len, chunk=cfg['ssd_chunk'], cdt=cdt)
    ssm_s = ssm_st.transpose(0, 1, 3, 2)
    n_tail = min(seq_len, SSM_CONV - 1)
    xbc_tail = proj.reshape(bsz, seq_len, -1)[:, seq_len - n_tail:, 5 * SSM_INNER:7 * SSM_INNER]
    conv_s = jnp.concatenate([conv_prev, xbc_tail], axis=1)[:, -(SSM_CONV - 1):]
    x = _outproj(x, o_a, y_b, w['w_out'], m1[2], vec(p['norm_post'][0, 1]), tm_pref=cfg['tm_mm'], **common)
    x = ffn(x, 0, 1, 2)

    x = ffn(x, 1, 0, 0)
    m1 = mods(1, 1)
    qraw, kvraw = _modmm(x, m1, vec(p['norm_pre'][1, 1]), w['w_dq'], w['w_dkv'],
                         tm_pref=cfg['tm_mm'], tn=MLA_Q_RANK, **common)
    q_scale = 1.0 if past is not None else MLA_SCALE * math.log2(math.e)
    qx, ckv, kpe, kx, kxt = _mlaq(qraw, kvraw, vec(p['mla_g_q']), vec(p['mla_g_kv']), w['w_uq'], w['w_uk'],
                                  pos, cdt=cdt, tm_pref=cfg['tm_mm'], q_scale=q_scale)
    if past is None:
        ol = _attn(qx, kx, kxt, bsz=bsz, seq_len=seq_len, tq=cfg['tq'], tk=cfg['tk'], hc=cfg['hc'])
    else:
        cache_c, cache_p, page_table = past
        o_dec = _decode(page_table, qx.transpose(1, 0, 2), ckv.reshape(bsz, 1, -1), kpe.reshape(bsz, 1, -1),
                        cache_c, cache_p.transpose(0, 2, 1), pg=cfg['pg'])
        ol = o_dec.transpose(1, 0, 2)
    x = _mlaout(x, ol, w['w_uv'], w['w_o'], m1[2], vec(p['norm_post'][1, 1]), tm_pref=cfg['tm_mm'], **common)
    x = ffn(x, 1, 1, 2)
    return x, hg_s, ssm_s, conv_s, ckv, kpe


def kernel(x_prompt, x_sample, c_prompt, c_sample, state_hgrn, state_ssm, state_conv, cache_ckv, cache_kpe, page_table, ada_w, ada_b, norm_pre, norm_post, ffn_wg, ffn_wu, ffn_wd, w_in_ab, w_out_ab, hg_lb_logits, hg_norm, ssm_conv_w, ssm_conv_b, ssm_dt_bias, ssm_a_log, ssm_d, ssm_norm, mla_w_dq, mla_g_q, mla_w_uq, mla_w_dkv, mla_g_kv, mla_w_uk, mla_w_uv, mla_w_o):
    p = dict(norm_pre=norm_pre, norm_post=norm_post, ffn_wg=ffn_wg, ffn_wu=ffn_wu, ffn_wd=ffn_wd,
             w_in_ab=w_in_ab, w_out_ab=w_out_ab, hg_lb_logits=hg_lb_logits, hg_norm=hg_norm,
             ssm_conv_w=ssm_conv_w, ssm_conv_b=ssm_conv_b, ssm_dt_bias=ssm_dt_bias, ssm_a_log=ssm_a_log,
             ssm_d=ssm_d, ssm_norm=ssm_norm, mla_w_dq=mla_w_dq, mla_g_q=mla_g_q, mla_w_uq=mla_w_uq,
             mla_w_dkv=mla_w_dkv, mla_g_kv=mla_g_kv, mla_w_uk=mla_w_uk, mla_w_uv=mla_w_uv, mla_w_o=mla_w_o)
    bp, seq, d = x_prompt.shape
    db, dseq, _ = x_sample.shape
    assert dseq == 1
    n_pages = page_table.shape[1]
    past_len = n_pages * PAGE_SIZE
    f = ffn_wg.shape[-1]

    head_of = np.arange(SSM_INNER) // SSM_HEAD_DIM
    emat = jnp.asarray((np.arange(LANES)[:, None] == head_of[None, :]).astype(np.float32), dtype=BF16)
    consts = dict(emat=emat)

    mod_all = _ada_mod(jnp.concatenate([c_prompt, c_sample], axis=0), ada_w, ada_b)

    fb = 2 * LANES if f % (2 * LANES) == 0 else f
    w_lo = _prep_weights(p, BF16)
    cfg_p = dict(tm_ffn=512, fb=fb, tm_mm=512, tm_in=256, l_pad=seq, hg_chunk=min(64, seq), hg_inner=max(1, min(8, seq // 64)),
                 hg_hps=1,
                 ssd_chunk=min(256, seq), tq=min(128, seq), tk=min(512, seq), hc=2)
    hg0 = jnp.zeros((bp, HG_HEADS, HG_KEY, HG_VAL), F32)
    ssm0 = jnp.zeros((bp, SSM_HEADS, SSM_STATE, SSM_HEAD_DIM), F32)
    conv0 = jnp.zeros((bp, SSM_CONV - 1, 2 * SSM_INNER), F32)
    pos_p = jnp.tile(_rope_tables(jnp.arange(seq, dtype=F32)), (bp, 1))
    y_p, hg_p, ssm_p, conv_p, ckv_p, kpe_p = _trunk(
        x_prompt.reshape(bp * seq, d), mod_all[:, :bp], pos_p, hg0, ssm0, conv0, None, p, w_lo, consts,
        bsz=bp, seq_len=seq, cdt=BF16, cfg=cfg_p)

    w_hi = _prep_weights(p, F32)
    pg = 32 if n_pages % 32 == 0 else n_pages
    cfg_s = dict(tm_ffn=128, fb=fb, tm_mm=128, tm_in=128, l_pad=SUBLANES, hg_chunk=SUBLANES, hg_inner=1, hg_hps=HG_HEADS,
                 ssd_chunk=SUBLANES, pg=pg)
    pos_s = _rope_tables(jnp.full((db,), past_len, F32))
    y_s, hg_s, ssm_s, conv_s, ckv_s, kpe_s = _trunk(
        x_sample.reshape(db, d), mod_all[:, bp:], pos_s, state_hgrn, state_ssm, state_conv,
        (cache_ckv, cache_kpe, page_table), p, w_hi, consts, bsz=db, seq_len=1, cdt=F32, cfg=cfg_s)

    return (y_p.reshape(bp, seq, d), y_s.reshape(db, 1, d), hg_p, hg_s, ssm_p, ssm_s, conv_p, conv_s,
            ckv_p.reshape(bp, seq, -1), ckv_s.reshape(db, 1, -1), kpe_p.reshape(bp, seq, -1),
            kpe_s.reshape(db, 1, -1))
```

```python
import functools
import math

import jax
import jax.numpy as jnp
import numpy as np
from jax import lax
from jax.experimental import pallas as pl
from jax.experimental.pallas import tpu as pltpu

F32 = jnp.float32
BF16 = jnp.bfloat16
HI = lax.Precision.HIGHEST

EPS = 1e-6
MACARON_W = 0.5
ROPE_THETA = 10000.0
PAGE_SIZE = 128

LANES = 128
SUBLANES = 8
VMEM_LIMIT = 56 * 1024 * 1024

HG_HEADS = 8
HG_KEY = 128
HG_VAL = 128
SSM_HEADS = 16
SSM_HEAD_DIM = 64
SSM_GROUPS = 4
SSM_STATE = 128
SSM_CONV = 4
SSM_INNER = SSM_HEADS * SSM_HEAD_DIM
SSM_GROUP_W = SSM_INNER // SSM_GROUPS
HEADS_PER_GROUP = SSM_HEADS // SSM_GROUPS
MLA_HEADS = 16
MLA_Q_RANK = 512
MLA_KV_RANK = 256
MLA_NOPE = 128
MLA_ROPE = 64
MLA_V = 128
MLA_SCALE = (MLA_NOPE + MLA_ROPE) ** -0.5
DECODE_SUB_PAGES = 8
FFN_AHEAD = 2
MLA_Q_AHEAD = 2
ATTN_AHEAD = 2
HG_SUB = SUBLANES


def _cparams(sem):
    return pltpu.CompilerParams(dimension_semantics=sem, vmem_limit_bytes=VMEM_LIMIT)


def _dot(a, b, cdt):
    if cdt == F32:
        return jnp.dot(a.astype(F32), b.astype(F32), precision=HI, preferred_element_type=F32)
    return jnp.dot(a.astype(BF16), b.astype(BF16), preferred_element_type=F32)


def _dot_nt(a, b, cdt):
    dn = (((1,), (1,)), ((), ()))
    if cdt == F32:
        return lax.dot_general(a.astype(F32), b.astype(F32), dn, precision=HI, preferred_element_type=F32)
    return lax.dot_general(a.astype(BF16), b.astype(BF16), dn, preferred_element_type=F32)


def _dot_hi(a, b):
    return jnp.dot(a, b, precision=HI, preferred_element_type=F32)


def _rms(x, g):
    r = lax.rsqrt(jnp.mean(x * x, axis=-1, keepdims=True) + EPS)
    return (x * r) * g


def _sigmoid(x):
    return 1.0 / (1.0 + jnp.exp(-x))


def _silu(x):
    return x * _sigmoid(x)


def _cumsum_rows(x):
    n = x.shape[0]
    row = lax.broadcasted_iota(jnp.int32, (n, 1), 0)
    s = 1
    while s < n:
        x = x + jnp.where(row >= s, pltpu.roll(x, s, 0), 0.0)
        s *= 2
    return x


def _expand_heads(a, e):
    hi = a.astype(BF16)
    r1 = a - hi.astype(F32)
    mid = r1.astype(BF16)
    lo = (r1 - mid.astype(F32)).astype(BF16)
    return (jnp.dot(hi, e, preferred_element_type=F32) + jnp.dot(mid, e, preferred_element_type=F32)
            + jnp.dot(lo, e, preferred_element_type=F32))


def _tile(n, pref):
    t = min(n, pref)
    while n % t:
        t -= SUBLANES
    return t


def _mod_spec(per_row, tm, tiles_per_seq, d):
    if per_row:
        return pl.BlockSpec((1, tm, d), lambda i, *_: (0, i, 0))
    return pl.BlockSpec((1, 1, d), lambda i, *_: (i // tiles_per_seq, 0, 0))


def _ada_kernel(c_ref, w_ref, b_ref, o_ref):
    cs = _silu(c_ref[...])
    o_ref[0] = _dot_hi(cs, w_ref[0]) + b_ref[0]


def _ada_mod(c_all, ada_w, ada_b):
    depth, d, n = ada_w.shape
    m = c_all.shape[0]
    tn = _tile(n, 1152)
    return pl.pallas_call(
        _ada_kernel,
        grid=(depth, n // tn),
        in_specs=[pl.BlockSpec((m, d), lambda l, j: (0, 0)),
                  pl.BlockSpec((1, d, tn), lambda l, j: (l, 0, j)),
                  pl.BlockSpec((1, 1, tn), lambda l, j: (l, 0, j))],
        out_specs=pl.BlockSpec((1, m, tn), lambda l, j: (l, 0, j)),
        out_shape=jax.ShapeDtypeStruct((depth, m, n), F32),
        compiler_params=_cparams(("parallel", "parallel")),
        name="ada_mod",
    )(c_all, ada_w, ada_b.reshape(depth, 1, n))


def _ffn_kernel(x_ref, gpre_ref, sh_ref, sc_ref, gt_ref, gpost_ref, wg_ref, wu_ref, wd_ref,
                o_ref, a_scr, *, cdt, fb):
    x = x_ref[...]
    h = (_rms(x, gpre_ref[...]) * (1.0 + sc_ref[0]) + sh_ref[0]).astype(cdt)
    nb = wg_ref.shape[1] // fb

    def gate_up(s):
        fs = slice(s * fb, (s + 1) * fb)
        return _dot(h, wg_ref[:, fs], cdt), _dot(h, wu_ref[:, fs], cdt)

    ahead = [gate_up(s) for s in range(min(FFN_AHEAD, nb))]
    for s in range(nb):
        g, u = ahead.pop(0)
        if s + FFN_AHEAD < nb:
            ahead.append(gate_up(s + FFN_AHEAD))
        a_scr[:, s * fb:(s + 1) * fb] = (_silu(g) * u).astype(cdt)
    y = _dot(a_scr[...], wd_ref[...], cdt)
    o_ref[...] = x + MACARON_W * gt_ref[0] * _rms(y, gpost_ref[...])


def _ffn(x, mods, gpre, gpost, wg, wu, wd, *, per_row, seq_len, cdt, tm_pref, fb):
    n, d = x.shape
    f = wg.shape[1]
    tm = _tile(seq_len if not per_row else n, tm_pref)
    tps = max(seq_len // tm, 1)
    shift, scale, gate = mods
    mspec = _mod_spec(per_row, tm, tps, d)
    vec = pl.BlockSpec((1, d), lambda i: (0, 0))
    whole = lambda a: pl.BlockSpec(a.shape, lambda i: (0, 0), pipeline_mode=pl.Buffered(1))
    return pl.pallas_call(
        functools.partial(_ffn_kernel, cdt=cdt, fb=fb),
        grid=(n // tm,),
        in_specs=[pl.BlockSpec((tm, d), lambda i: (i, 0)), vec, mspec, mspec, mspec, vec,
                  whole(wg), whole(wu), whole(wd)],
        out_specs=pl.BlockSpec((tm, d), lambda i: (i, 0)),
        out_shape=jax.ShapeDtypeStruct((n, d), F32),
        scratch_shapes=[pltpu.VMEM((tm, f), cdt)],
        compiler_params=_cparams(("parallel",)),
        name="ffn",
    )(x, gpre, shift, scale, gate, gpost, wg, wu, wd)


def _modmm_kernel(x_ref, gpre_ref, sh_ref, sc_ref, w_ref, wx_ref, o_ref, ox_ref, *, cdt, tn):
    h = (_rms(x_ref[...], gpre_ref[...]) * (1.0 + sc_ref[0]) + sh_ref[0]).astype(cdt)
    ox_ref[...] = _dot(h, wx_ref[...], cdt)
    for jb in range(w_ref.shape[1] // tn):
        cs = slice(jb * tn, (jb + 1) * tn)
        o_ref[:, cs] = _dot(h, w_ref[:, cs], cdt)


def _modmm(x, mods, gpre, w, wx, *, per_row, seq_len, cdt, tm_pref, tn):
    n, d = x.shape
    nout = w.shape[1]
    nx = wx.shape[1]
    tm = _tile(seq_len if not per_row else n, tm_pref)
    tps = max(seq_len // tm, 1)
    shift, scale, _ = mods
    mspec = _mod_spec(per_row, tm, tps, d)
    whole = lambda a: pl.BlockSpec(a.shape, lambda i: (0, 0), pipeline_mode=pl.Buffered(1))
    return pl.pallas_call(
        functools.partial(_modmm_kernel, cdt=cdt, tn=tn),
        grid=(n // tm,),
        in_specs=[pl.BlockSpec((tm, d), lambda i: (i, 0)),
                  pl.BlockSpec((1, d), lambda i: (0, 0)), mspec, mspec, whole(w), whole(wx)],
        out_specs=[pl.BlockSpec((tm, nout), lambda i: (i, 0)),
                   pl.BlockSpec((tm, nx), lambda i: (i, 0))],
        out_shape=[jax.ShapeDtypeStruct((n, nout), F32), jax.ShapeDtypeStruct((n, nx), F32)],
        compiler_params=_cparams(("parallel",)),
        name="modmm",
    )(x, gpre, shift, scale, w, wx)


def _hgrn_kernel(q_ref, f_ref, v_ref, g_ref, lb_ref, gn_ref, s0_ref, o_ref, sout_ref, st_scr,
                 *, chunk, n_inner, hps, l_true, nl, cdt):
    l = pl.program_id(2)
    t_blk = chunk * n_inner
    sub = HG_SUB
    nsub = chunk // sub

    @pl.when(l == 0)
    def _():
        for hh in range(hps):
            st_scr[hh] = s0_ref[0, hh].T

    rowc = lax.broadcasted_iota(jnp.int32, (chunk, 1), 0)
    tril = (lax.broadcasted_iota(jnp.int32, (chunk, chunk), 0)
            >= lax.broadcasted_iota(jnp.int32, (chunk, chunk), 1)).astype(F32)
    srow = lax.broadcasted_iota(jnp.int32, (nsub, sub, 1), 1)
    blocks = lambda a: a.reshape(nsub, sub, a.shape[-1])

    for hh in range(hps):
        cols = slice(hh * HG_KEY, (hh + 1) * HG_KEY)
        lbh = lb_ref[:, cols]
        gn = gn_ref[:, cols]
        pre = []
        for c in range(n_inner):
            rows = slice(c * chunk, (c + 1) * chunk)
            valid = (l * t_blk + c * chunk + rowc) < l_true
            fr = f_ref[rows, cols]
            v = v_ref[rows, cols]
            f = lbh + (1.0 - lbh) * _sigmoid(fr)
            logf = jnp.where(valid, jnp.log(f), 0.0)
            k = jnp.where(valid, (1.0 - lbh) * _sigmoid(-fr), 0.0)
            q = _silu(q_ref[rows, cols])
            b = _dot_hi(tril, logf)
            pre.append((rows, q, k, v, b))
        atts = []
        for rows, q, k, v, b in pre:
            att_c = []
            for i in range(1, nsub):
                lo = i * sub
                hi = min(chunk, -(-lo // 16) * 16)
                ref_b = b[lo - 1:lo, :]
                qi = q[lo:lo + sub, :] * jnp.exp(b[lo:lo + sub, :] - ref_b)
                ki = jnp.where(rowc[0:hi] < lo, k[0:hi, :] * jnp.exp(ref_b - b[0:hi, :]), 0.0)
                att_c.append((_dot_nt(qi, ki, cdt), hi))
            atts.append(att_c)
        intra, incs = [], []
        for (rows, q, k, v, b), att_c in zip(pre, atts):
            q3, k3, b3, v3 = blocks(q), blocks(k), blocks(b), blocks(v)
            o3 = jnp.sum(q3 * k3, axis=-1, keepdims=True) * v3
            for d in range(1, sub):
                k_sh = pltpu.roll(k3, d, 1)
                b_sh = pltpu.roll(b3, d, 1)
                v_sh = pltpu.roll(v3, d, 1)
                a = jnp.sum(q3 * k_sh * jnp.exp(b3 - b_sh), axis=-1, keepdims=True)
                o3 = o3 + jnp.where(srow >= d, a, 0.0) * v_sh
            o = o3.reshape(chunk, HG_VAL)
            if nsub > 1:
                parts = [jnp.zeros((sub, HG_VAL), F32)] + [_dot(att, v[0:hi, :], cdt) for att, hi in att_c]
                o = o + jnp.concatenate(parts, axis=0)
            intra.append(o)
            bl = b[chunk - 1:chunk, :]
            incs.append(_dot(v.T, k * jnp.exp(bl - b), cdt))
        st = st_scr[hh]
        for (rows, q, k, v, b), o, inc in zip(pre, intra, incs):
            o = o + _dot_nt(q * jnp.exp(b), st, cdt)
            st = st * jnp.exp(b[chunk - 1:chunk, :]) + inc
            o_ref[rows, cols] = _rms(o, gn) * _silu(g_ref[rows, cols])
        st_scr[hh] = st

    @pl.when(l == nl - 1)
    def _():
        for hh in range(hps):
            sout_ref[0, hh] = st_scr[hh].T


def _hgrn(proj, lb, gn, s0, *, bsz, l_pad, l_true, chunk, n_inner, hps, cdt):
    t_blk = chunk * n_inner
    nl = l_pad // t_blk
    n = bsz * l_pad
    nhb = HG_HEADS // hps
    wide = hps * HG_KEY

    def col(cb):
        return pl.BlockSpec((t_blk, wide), lambda b, h, l: (b * nl + l, cb * nhb + h))

    hvec = pl.BlockSpec((1, wide), lambda b, h, l: (0, h))
    sspec = pl.BlockSpec((1, hps, HG_KEY, HG_VAL), lambda b, h, l: (b, h, 0, 0))
    return pl.pallas_call(
        functools.partial(_hgrn_kernel, chunk=chunk, n_inner=n_inner, hps=hps, l_true=l_true, nl=nl, cdt=cdt),
        grid=(bsz, nhb, nl),
        in_specs=[col(0), col(1), col(2), col(3), hvec, hvec, sspec],
        out_specs=[pl.BlockSpec((t_blk, wide), lambda b, h, l: (b * nl + l, h)), sspec],
        out_shape=[jax.ShapeDtypeStruct((n, HG_HEADS * HG_VAL), F32),
                   jax.ShapeDtypeStruct(s0.shape, F32)],
        scratch_shapes=[pltpu.VMEM((hps, HG_VAL, HG_KEY), F32)],
        compiler_params=_cparams(("parallel", "parallel", "arbitrary")),
        name="hgrn2",
    )(proj, proj, proj, proj, lb, gn, s0)


def _hgrn_step_kernel(q_ref, f_ref, v_ref, g_ref, lb_ref, gn_ref, s0_ref, o_ref, sout_ref):
    nr = q_ref.shape[0]
    for h in range(HG_HEADS):
        cols = slice(h * HG_KEY, (h + 1) * HG_KEY)
        lbh = lb_ref[:, cols]
        fr = f_ref[:, cols]
        f_t = (lbh + (1.0 - lbh) * _sigmoid(fr)).T
        k_t = ((1.0 - lbh) * _sigmoid(-fr)).T
        q_t = _silu(q_ref[:, cols]).T
        v = v_ref[:, cols]
        rows = []
        for r in range(nr):
            s_new = f_t[:, r:r + 1] * s0_ref[r, h] + k_t[:, r:r + 1] * v[r:r + 1, :]
            sout_ref[r, h] = s_new
            rows.append(jnp.sum(q_t[:, r:r + 1] * s_new, axis=0, keepdims=True))
        o = jnp.concatenate(rows, axis=0)
        o_ref[:, cols] = _rms(o, gn_ref[:, cols]) * _silu(g_ref[:, cols])


def _hgrn_step(proj, lb, gn, s0, *, nr):
    n = proj.shape[0]
    wide = HG_HEADS * HG_KEY
    col = lambda cb: pl.BlockSpec((nr, wide), lambda i: (i, cb))
    vecs = pl.BlockSpec((1, wide), lambda i: (0, 0))
    sspec = pl.BlockSpec((nr, HG_HEADS, HG_KEY, HG_VAL), lambda i: (i, 0, 0, 0))
    return pl.pallas_call(
        _hgrn_step_kernel,
        grid=(n // nr,),
        in_specs=[col(0), col(1), col(2), col(3), vecs, vecs, sspec],
        out_specs=[pl.BlockSpec((nr, wide), lambda i: (i, 0)), sspec],
        out_shape=[jax.ShapeDtypeStruct((n, wide), F32), jax.ShapeDtypeStruct(s0.shape, F32)],
        compiler_params=_cparams(("parallel",)),
        name="hgrn2_step",
    )(proj, proj, proj, proj, lb, gn, s0)


def _ssd_step_kernel(z_ref, xa_ref, xb_ref, dt_ref, cw_ref, cb_ref, dtb_ref, alog_ref, dexp_ref, nrm_ref,
                     e_ref, cprev_ref, s0_ref, y_ref, sout_ref):
    nr = z_ref.shape[0]
    gw = SSM_GROUP_W
    xraw = jnp.concatenate([xa_ref[...], xb_ref[...]], axis=-1)
    conv = cb_ref[...] + cw_ref[SSM_CONV - 1:SSM_CONV, :] * xraw
    for i in range(SSM_CONV - 1):
        conv = conv + cw_ref[i:i + 1, :] * cprev_ref[:, i, :]
    xc = _silu(conv)
    xh = xc[:, 0:SSM_INNER]
    bm = xc[:, SSM_INNER:SSM_INNER + SSM_GROUPS * SSM_STATE]
    cm = xc[:, SSM_INNER + SSM_GROUPS * SSM_STATE:]
    draw = dt_ref[...] + dtb_ref[...]
    dt = jnp.maximum(draw, 0.0) + jnp.log1p(jnp.exp(-jnp.abs(draw)))
    e = e_ref[...]
    dec_t = _expand_heads(jnp.exp(dt * (-jnp.exp(alog_ref[...]))), e).T
    xd_t = (xh * _expand_heads(dt, e)).T
    for h in range(SSM_HEADS):
        g = h // HEADS_PER_GROUP
        hs = slice(h * SSM_HEAD_DIM, (h + 1) * SSM_HEAD_DIM)
        bg = bm[:, g * SSM_STATE:(g + 1) * SSM_STATE]
        for r in range(nr):
            sout_ref[r, h] = dec_t[hs, r:r + 1] * s0_ref[r, h] + xd_t[hs, r:r + 1] * bg[r:r + 1, :]
    y_parts = []
    for h in range(SSM_HEADS):
        g = h // HEADS_PER_GROUP
        cg = cm[:, g * SSM_STATE:(g + 1) * SSM_STATE]
        cols = [jnp.sum(sout_ref[r, h] * cg[r:r + 1, :], axis=-1, keepdims=True) for r in range(nr)]
        y_parts.append(jnp.concatenate(cols, axis=-1))
    y = jnp.concatenate(y_parts, axis=0).T
    y = (y + dexp_ref[...] * xh) * _silu(z_ref[...])
    outs = []
    for g in range(SSM_GROUPS):
        gs = slice(g * gw, (g + 1) * gw)
        outs.append(_rms(y[:, gs], nrm_ref[:, gs]))
    y_ref[...] = jnp.concatenate(outs, axis=-1)


def _ssd_step(proj, dtraw, cw, cb, dtb, alog, dexp, nrm, emat, cprev, s0t, *, nr):
    n = proj.shape[0]
    w = SSM_INNER
    zcol = 4 * HG_HEADS * HG_KEY // w
    cc = 2 * w
    col = lambda cbk: pl.BlockSpec((nr, w), lambda i: (i, cbk))
    const = lambda shape: pl.BlockSpec(shape, lambda i: (0,) * len(shape))
    sspec = pl.BlockSpec((nr, SSM_HEADS, SSM_HEAD_DIM, SSM_STATE), lambda i: (i, 0, 0, 0))
    return pl.pallas_call(
        _ssd_step_kernel,
        grid=(n // nr,),
        in_specs=[col(zcol), col(zcol + 1), col(zcol + 2), pl.BlockSpec((nr, LANES), lambda i: (i, 0)),
                  const((SSM_CONV, cc)), const((1, cc)), const((1, LANES)), const((1, LANES)),
                  const((1, w)), const((1, w)), const((LANES, w)),
                  pl.BlockSpec((nr, SSM_CONV - 1, cc), lambda i: (i, 0, 0)), sspec],
        out_specs=[pl.BlockSpec((nr, w), lambda i: (i, 0)), sspec],
        out_shape=[jax.ShapeDtypeStruct((n, w), F32), jax.ShapeDtypeStruct(s0t.shape, F32)],
        compiler_params=_cparams(("parallel",)),
        name="ssd_step",
    )(proj, proj, proj, dtraw, cw, cb, dtb, alog, dexp, nrm, emat, cprev, s0t)


def _ssd_kernel(z_ref, xa_ref, xb_ref, dt_ref, cw_ref, cb_ref, dtb_ref, alog_ref, dexp_ref, nrm_ref,
                e_ref, cprev_ref, s0_ref, y_ref, sout_ref, xpad_scr, s_scr, *, chunk, l_true, nl, cdt):
    l = pl.program_id(1)
    c = chunk
    gw = SSM_GROUP_W

    @pl.when(l == 0)
    def _():
        xpad_scr[0:SUBLANES, :] = cprev_ref[0]
        for g in range(SSM_GROUPS):
            s_scr[g] = s0_ref[0, g * HEADS_PER_GROUP:(g + 1) * HEADS_PER_GROUP].reshape(gw, SSM_STATE)

    @pl.when(l > 0)
    def _():
        xpad_scr[0:SUBLANES, :] = xpad_scr[c:c + SUBLANES, :]

    xpad_scr[SUBLANES:SUBLANES + c, 0:SSM_INNER] = xa_ref[...]
    xpad_scr[SUBLANES:SUBLANES + c, SSM_INNER:2 * SSM_INNER] = xb_ref[...]
    conv = cb_ref[...]
    for i in range(SSM_CONV):
        lo = SUBLANES - (SSM_CONV - 1) + i
        conv = conv + cw_ref[i:i + 1, :] * xpad_scr[lo:lo + c, :]
    xc = _silu(conv)
    xh = xc[:, 0:SSM_INNER]
    bm = xc[:, SSM_INNER:SSM_INNER + SSM_GROUPS * SSM_STATE]
    cm = xc[:, SSM_INNER + SSM_GROUPS * SSM_STATE:]

    rowc = lax.broadcasted_iota(jnp.int32, (c, 1), 0)
    valid = (l * c + rowc) < l_true
    draw = dt_ref[...] + dtb_ref[...]
    dt = jnp.maximum(draw, 0.0) + jnp.log1p(jnp.exp(-jnp.abs(draw)))
    dt = jnp.where(valid, dt, 0.0)
    la = dt * (-jnp.exp(alog_ref[...]))
    ri = lax.broadcasted_iota(jnp.int32, (c, c), 0)
    ci = lax.broadcasted_iota(jnp.int32, (c, c), 1)
    causal = ri >= ci
    b = _cumsum_rows(la)
    b_t = b.T
    dt_t = dt.T
    bl = b[c - 1:c, :]
    expand = functools.partial(_expand_heads, e=e_ref[...])

    lane = lax.broadcasted_iota(jnp.int32, (1, gw), 1)
    grp = lambda a, g: a[:, g * SSM_STATE:(g + 1) * SSM_STATE]
    gsl = lambda g: slice(g * gw, (g + 1) * gw)
    cbs = [_dot_nt(grp(cm, g), grp(bm, g), cdt) for g in range(SSM_GROUPS)]
    carried = [_dot_nt(grp(cm, g), s_scr[g], cdt) for g in range(SSM_GROUPS)]
    eb_e = expand(jnp.exp(b))
    dtw_e = expand(dt * jnp.exp(bl - b))
    for g in range(SSM_GROUPS):
        gs = gsl(g)
        decay = jnp.broadcast_to(eb_e[c - 1:c, gs], (SUBLANES, gw)).T[:, 0:1]
        s_scr[g] = decay * s_scr[g] + _dot((xh[:, gs] * dtw_e[:, gs]).T, grp(bm, g), cdt)
    ys = []
    for g in range(SSM_GROUPS):
        gs = gsl(g)
        yg = carried[g] * eb_e[:, gs]
        for j in range(HEADS_PER_GROUP):
            h = g * HEADS_PER_GROUP + j
            seg = jnp.exp(jnp.where(causal, b[:, h:h + 1] - b_t[h:h + 1, :], -jnp.inf)) * dt_t[h:h + 1, :]
            in_head = (lane >= j * SSM_HEAD_DIM) & (lane < (j + 1) * SSM_HEAD_DIM)
            yg = yg + _dot(cbs[g] * seg, jnp.where(in_head, xh[:, gs], 0.0), cdt)
        ys.append(yg)
    y = jnp.concatenate(ys, axis=-1)
    y = (y + dexp_ref[...] * xh) * _silu(z_ref[...])
    outs = []
    for g in range(SSM_GROUPS):
        gs = slice(g * gw, (g + 1) * gw)
        outs.append(_rms(y[:, gs], nrm_ref[:, gs]))
    y_ref[...] = jnp.concatenate(outs, axis=-1)

    @pl.when(l == nl - 1)
    def _():
        for g in range(SSM_GROUPS):
            sout_ref[0, g * HEADS_PER_GROUP:(g + 1) * HEADS_PER_GROUP] = s_scr[g].reshape(
                HEADS_PER_GROUP, SSM_HEAD_DIM, SSM_STATE)


def _ssd(proj, dtraw, cw, cb, dtb, alog, dexp, nrm, emat, cprev, s0g, *, bsz, l_pad, l_true, chunk, cdt):
    nl = l_pad // chunk
    n = bsz * l_pad
    w = SSM_INNER
    zcol = 4 * HG_HEADS * HG_KEY // w
    cc = 2 * w

    def col(cbk):
        return pl.BlockSpec((chunk, w), lambda b, l: (b * nl + l, cbk))

    def const(shape):
        return pl.BlockSpec(shape, lambda b, l: (0,) * len(shape))

    sspec = pl.BlockSpec((1, SSM_HEADS, SSM_HEAD_DIM, SSM_STATE), lambda b, l: (b, 0, 0, 0))
    return pl.pallas_call(
        functools.partial(_ssd_kernel, chunk=chunk, l_true=l_true, nl=nl, cdt=cdt),
        grid=(bsz, nl),
        in_specs=[col(zcol), col(zcol + 1), col(zcol + 2),
                  pl.BlockSpec((chunk, LANES), lambda b, l: (b * nl + l, 0)),
                  const((SSM_CONV, cc)), const((1, cc)), const((1, LANES)), const((1, LANES)),
                  const((1, w)), const((1, w)), const((LANES, w)),
                  pl.BlockSpec((1, SUBLANES, cc), lambda b, l: (b, 0, 0)), sspec],
        out_specs=[pl.BlockSpec((chunk, w), lambda b, l: (b * nl + l, 0)), sspec],
        out_shape=[jax.ShapeDtypeStruct((n, w), F32), jax.ShapeDtypeStruct(s0g.shape, F32)],
        scratch_shapes=[pltpu.VMEM((chunk + 2 * SUBLANES, cc), F32),
                        pltpu.VMEM((SSM_GROUPS, SSM_GROUP_W, SSM_STATE), F32)],
        compiler_params=_cparams(("parallel", "arbitrary")),
        name="ssd",
    )(proj, proj, proj, dtraw, cw, cb, dtb, alog, dexp, nrm, emat, cprev, s0g)


def _outproj_kernel(x_ref, a_ref, b_ref, w_ref, gt_ref, gpost_ref, o_ref, *, cdt):
    ka = a_ref.shape[1]
    y = _dot(a_ref[...], w_ref[0:ka, :], cdt) + _dot(b_ref[...], w_ref[ka:, :], cdt)
    o_ref[...] = x_ref[...] + gt_ref[0] * _rms(y, gpost_ref[...])


def _outproj(x, a, b, w, gate, gpost, *, per_row, seq_len, cdt, tm_pref):
    n, d = x.shape
    tm = _tile(seq_len if not per_row else n, tm_pref)
    tps = max(seq_len // tm, 1)
    mspec = _mod_spec(per_row, tm, tps, d)
    row = lambda width: pl.BlockSpec((tm, width), lambda i: (i, 0))
    return pl.pallas_call(
        functools.partial(_outproj_kernel, cdt=cdt),
        grid=(n // tm,),
        in_specs=[row(d), row(a.shape[1]), row(b.shape[1]),
                  pl.BlockSpec(w.shape, lambda i: (0, 0)), mspec,
                  pl.BlockSpec((1, d), lambda i: (0, 0))],
        out_specs=row(d),
        out_shape=jax.ShapeDtypeStruct((n, d), F32),
        compiler_params=_cparams(("parallel",)),
        name="outproj",
    )(x, a, b, w, gate, gpost)


def _rope(xs, cs_t):
    u = xs * cs_t
    return u + pltpu.roll(u, MLA_ROPE, 1)


def _mlaq_kernel(qr_ref, kv_ref, gq_ref, gkv_ref, wuq_ref, wuk_ref, cs_ref,
                 qx_ref, ckv_ref, kpe_ref, kx_ref, kxt_ref, *, cdt, q_scale):
    cs_t = cs_ref[...]
    lane = lax.broadcasted_iota(jnp.int32, (1, 2 * MLA_ROPE), 1)
    low = lane < MLA_ROPE
    cq = _rms(qr_ref[...], gq_ref[...]).astype(cdt)
    kv = kv_ref[...]
    ckv = _rms(kv[:, 0:MLA_KV_RANK], gkv_ref[...])
    kpe = _rope(kv[:, MLA_KV_RANK:MLA_KV_RANK + 2 * MLA_ROPE], cs_t)
    ckv_ref[...] = ckv
    kpe_ref[...] = kpe[:, 0:MLA_ROPE]
    kx = jnp.concatenate([ckv, jnp.where(low, kpe, 0.0)], axis=-1)
    kx_ref[...] = kx.astype(kx_ref.dtype)
    kxt_ref[...] = kx.T.astype(kxt_ref.dtype)
    up = lambda h: _dot(cq, wuq_ref[h], cdt)
    ahead = [up(h) for h in range(MLA_Q_AHEAD)]
    for h in range(MLA_HEADS):
        q = ahead.pop(0)
        if h + MLA_Q_AHEAD < MLA_HEADS:
            ahead.append(up(h + MLA_Q_AHEAD))
        ql = _dot(q[:, 0:MLA_NOPE], wuk_ref[h], cdt)
        qp = _rope(q[:, MLA_NOPE:MLA_NOPE + 2 * MLA_ROPE], cs_t)
        qx_ref[h, :, 0:MLA_KV_RANK] = (ql * q_scale).astype(qx_ref.dtype)
        qx_ref[h, :, MLA_KV_RANK:] = jnp.where(low, qp * q_scale, 0.0).astype(qx_ref.dtype)


def _mlaq(qraw, kvraw, gq, gkv, wuq, wuk, cs_t, *, cdt, tm_pref, q_scale):
    n = qraw.shape[0]
    tm = _tile(n, tm_pref)
    h = MLA_HEADS
    kw = MLA_KV_RANK + 2 * MLA_ROPE
    row = lambda width: pl.BlockSpec((tm, width), lambda i: (i, 0))
    const = lambda shape: pl.BlockSpec(shape, lambda i: (0,) * len(shape))
    return pl.pallas_call(
        functools.partial(_mlaq_kernel, cdt=cdt, q_scale=q_scale),
        grid=(n // tm,),
        in_specs=[row(MLA_Q_RANK), row(kvraw.shape[1]), const((1, MLA_Q_RANK)), const((1, MLA_KV_RANK)),
                  const(wuq.shape), const(wuk.shape), row(2 * MLA_ROPE)],
        out_specs=[pl.BlockSpec((h, tm, kw), lambda i: (0, i, 0)), row(MLA_KV_RANK), row(MLA_ROPE), row(kw),
                   pl.BlockSpec((kw, tm), lambda i: (0, i))],
        out_shape=[jax.ShapeDtypeStruct((h, n, kw), cdt),
                   jax.ShapeDtypeStruct((n, MLA_KV_RANK), F32),
                   jax.ShapeDtypeStruct((n, MLA_ROPE), F32),
                   jax.ShapeDtypeStruct((n, kw), BF16),
                   jax.ShapeDtypeStruct((kw, n), BF16)],
        compiler_params=_cparams(("parallel",)),
        name="mla_q",
    )(qraw, kvraw, gq, gkv, wuq, wuk, cs_t)


def _attn_kernel(it_ref, jt_ref, qx_ref, kx_ref, kxt_ref, o_ref, m_scr, l_scr, acc_scr, *, tq, tk, hc):
    step = pl.program_id(1)
    i = it_ref[step]
    j = jt_ref[step]
    last = (i * tq + tq - 1) // tk

    @pl.when(j == 0)
    def _():
        m_scr[...] = jnp.full_like(m_scr, -jnp.inf)
        l_scr[...] = jnp.zeros_like(l_scr)
        acc_scr[...] = jnp.zeros_like(acc_scr)

    rc = hc * tq
    kw = qx_ref.shape[2]

    def key_tile(masked, width):
        kc = kx_ref[0:width, 0:MLA_KV_RANK]
        if masked:
            qpos = i * tq + (lax.broadcasted_iota(jnp.int32, (rc, width), 0) & (tq - 1))
            kpos = j * tk + lax.broadcasted_iota(jnp.int32, (rc, width), 1)
            keep = kpos <= qpos
        logits = lambda c: _dot(qx_ref[c * hc:(c + 1) * hc].reshape(rc, kw), kxt_ref[:, 0:width], BF16)
        nblk = MLA_HEADS // hc
        ahead = [logits(c) for c in range(min(ATTN_AHEAD, nblk))]
        for c in range(nblk):
            rs = slice(c * rc, (c + 1) * rc)
            s = ahead.pop(0)
            if c + ATTN_AHEAD < nblk:
                ahead.append(logits(c + ATTN_AHEAD))
            if masked:
                s = jnp.where(keep, s, -jnp.inf)
            m_prev = m_scr[rs]
            m_new = jnp.maximum(m_prev, jnp.max(s, axis=-1, keepdims=True))
            alpha = jnp.exp2(m_prev - m_new)
            p = jnp.exp2(s - m_new)
            l_scr[rs] = alpha * l_scr[rs] + jnp.sum(p, axis=-1, keepdims=True)
            acc_scr[rs] = alpha * acc_scr[rs] + _dot(p, kc, BF16)
            m_scr[rs] = m_new

    @pl.when(j < last)
    def _():
        key_tile(False, tk)

    @pl.when(j == last)
    def _():
        for v in range(tk // tq):
            @pl.when(i * tq - j * tk == v * tq)
            def _():
                key_tile(True, (v + 1) * tq)
        o = acc_scr[...] / l_scr[...]
        o_ref[...] = o.reshape(MLA_HEADS, tq, MLA_KV_RANK).astype(o_ref.dtype)


def _attn(qx, kx, kxt, *, bsz, seq_len, tq, tk, hc):
    h, n, kw = qx.shape
    r = MLA_KV_RANK
    nq = seq_len // tq
    nk = seq_len // tk
    rows = h * tq
    pairs = [(i, j) for i in range(nq) for j in range((i * tq + tq - 1) // tk + 1)]
    i_of = jnp.asarray([ij[0] for ij in pairs], jnp.int32)
    j_of = jnp.asarray([ij[1] for ij in pairs], jnp.int32)
    grid_spec = pltpu.PrefetchScalarGridSpec(
        num_scalar_prefetch=2,
        grid=(bsz, len(pairs)),
        in_specs=[pl.BlockSpec((h, tq, kw), lambda b, s, it, jt: (0, b * nq + it[s], 0)),
                  pl.BlockSpec((tk, kw), lambda b, s, it, jt: (b * nk + jt[s], 0)),
                  pl.BlockSpec((kw, tk), lambda b, s, it, jt: (0, b * nk + jt[s]))],
        out_specs=pl.BlockSpec((h, tq, r), lambda b, s, it, jt: (0, b * nq + it[s], 0)),
        scratch_shapes=[pltpu.VMEM((rows, 1), F32), pltpu.VMEM((rows, 1), F32), pltpu.VMEM((rows, r), F32)],
    )
    return pl.pallas_call(
        functools.partial(_attn_kernel, tq=tq, tk=tk, hc=hc),
        grid_spec=grid_spec,
        out_shape=jax.ShapeDtypeStruct((h, n, r), BF16),
        compiler_params=_cparams(("parallel", "arbitrary")),
        name="mla_attn",
    )(i_of, j_of, qx, kx, kxt)


def _decode_kernel(pt_ref, qx_ref, ckn_ref, kpn_ref, cache_c, cache_p, o_ref, cbuf, pbuf, sems,
                   *, n_pages, pg, sub_pg, n_seq):
    bidx = pl.program_id(0)
    n_chunks = n_pages // pg
    total_chunks = n_seq * n_chunks
    sub = sub_pg * PAGE_SIZE

    def page_copies(page, slot, p):
        return (pltpu.make_async_copy(cache_c.at[page], cbuf.at[slot, p], sems.at[0, slot]),
                pltpu.make_async_copy(cache_p.at[page], pbuf.at[slot, p], sems.at[1, slot]))

    def start(g, slot):
        for p in range(pg):
            for cp in page_copies(pt_ref[g * pg + p], slot, p):
                cp.start()

    def wait(slot):
        for p in range(pg):
            for cp in page_copies(0, slot, p):
                cp.wait()

    qx = qx_ref[0]
    ql = qx[:, 0:MLA_KV_RANK]
    qp = qx[:, MLA_KV_RANK:MLA_KV_RANK + MLA_ROPE]

    @pl.when(bidx == 0)
    def _():
        start(0, 0)

    def body(cidx, carry):
        m_prev, l_prev, acc = carry
        g = bidx * n_chunks + cidx
        slot = g % 2

        start(jnp.minimum(g + 1, total_chunks - 1), 1 - slot)
        wait(slot)
        kcs, s_parts = [], []
        for sb in range(pg // sub_pg):
            kc = cbuf[slot, sb * sub_pg:(sb + 1) * sub_pg].reshape(sub, MLA_KV_RANK).astype(BF16)
            kcs.append(kc)
            s_rope = jnp.concatenate(
                [_dot(qp, pbuf[slot, sb * sub_pg + p], BF16) for p in range(sub_pg)], axis=-1)
            s_parts.append(_dot_nt(ql, kc, BF16) + s_rope)
        s = jnp.concatenate(s_parts, axis=-1) * MLA_SCALE
        m_new = jnp.maximum(m_prev, jnp.max(s, axis=-1, keepdims=True))
        alpha = jnp.exp(m_prev - m_new)
        p = jnp.exp(s - m_new)
        l_new = alpha * l_prev + jnp.sum(p, axis=-1, keepdims=True)
        pv = _dot(p[:, 0:sub], kcs[0], BF16)
        for sb in range(1, pg // sub_pg):
            pv = pv + _dot(p[:, sb * sub:(sb + 1) * sub], kcs[sb], BF16)
        return m_new, l_new, alpha * acc + pv

    init = (jnp.full((MLA_HEADS, 1), -jnp.inf, F32), jnp.zeros((MLA_HEADS, 1), F32),
            jnp.zeros((MLA_HEADS, MLA_KV_RANK), F32))
    m_prev, l_prev, acc = lax.fori_loop(0, n_chunks, body, init)

    @pl.when(bidx == n_seq - 1)
    def _():
        wait(total_chunks % 2)

    ckn = ckn_ref[0]
    s_own = (jnp.sum(ql * ckn, axis=-1, keepdims=True)
             + jnp.sum(qp * kpn_ref[0], axis=-1, keepdims=True)) * MLA_SCALE
    m_new = jnp.maximum(m_prev, s_own)
    alpha = jnp.exp(m_prev - m_new)
    p_own = jnp.exp(s_own - m_new)
    l_new = alpha * l_prev + p_own
    o_ref[0] = (alpha * acc + p_own * ckn) / l_new


def _decode(page_table, qx, ckn, kpn, cache_c, cache_pt, *, pg):
    db, n_pages = page_table.shape
    h, r, dr = MLA_HEADS, MLA_KV_RANK, MLA_ROPE
    grid_spec = pltpu.PrefetchScalarGridSpec(
        num_scalar_prefetch=1,
        grid=(db,),
        in_specs=[pl.BlockSpec((1, h, qx.shape[2]), lambda b, pt: (b, 0, 0)),
                  pl.BlockSpec((1, 1, r), lambda b, pt: (b, 0, 0)),
                  pl.BlockSpec((1, 1, dr), lambda b, pt: (b, 0, 0)),
                  pl.BlockSpec(memory_space=pl.ANY),
                  pl.BlockSpec(memory_space=pl.ANY)],
        out_specs=pl.BlockSpec((1, h, r), lambda b, pt: (b, 0, 0)),
        scratch_shapes=[pltpu.VMEM((2, pg, PAGE_SIZE, r), F32),
                        pltpu.VMEM((2, pg, dr, PAGE_SIZE), F32),
                        pltpu.SemaphoreType.DMA((2, 2))],
    )
    return pl.pallas_call(
        functools.partial(_decode_kernel, n_pages=n_pages, pg=pg, sub_pg=min(DECODE_SUB_PAGES, pg), n_seq=db),
        grid_spec=grid_spec,
        out_shape=jax.ShapeDtypeStruct((db, h, r), F32),
        compiler_params=_cparams(("arbitrary",)),
        name="mla_decode",
    )(page_table.reshape(-1), qx, ckn, kpn, cache_c, cache_pt)


def _mlaout_kernel(x_ref, ol_ref, wuv_ref, wo_ref, gt_ref, gpost_ref, o_ref, *, cdt):
    t = jnp.concatenate([_dot(ol_ref[h], wuv_ref[h], cdt).astype(cdt) for h in range(MLA_HEADS)], axis=-1)
    y = _dot(t, wo_ref[...], cdt)
    o_ref[...] = x_ref[...] + gt_ref[0] * _rms(y, gpost_ref[...])


def _mlaout(x, ol, wuv, wo, gate, gpost, *, per_row, seq_len, cdt, tm_pref):
    n, d = x.shape
    tm = _tile(seq_len if not per_row else n, tm_pref)
    tps = max(seq_len // tm, 1)
    mspec = _mod_spec(per_row, tm, tps, d)
    return pl.pallas_call(
        functools.partial(_mlaout_kernel, cdt=cdt),
        grid=(n // tm,),
        in_specs=[pl.BlockSpec((tm, d), lambda i: (i, 0)),
                  pl.BlockSpec((MLA_HEADS, tm, MLA_KV_RANK), lambda i: (0, i, 0)),
                  pl.BlockSpec(wuv.shape, lambda i: (0, 0, 0)),
                  pl.BlockSpec(wo.shape, lambda i: (0, 0)),
                  mspec, pl.BlockSpec((1, d), lambda i: (0, 0))],
        out_specs=pl.BlockSpec((tm, d), lambda i: (i, 0)),
        out_shape=jax.ShapeDtypeStruct((n, d), F32),
        compiler_params=_cparams(("parallel",)),
        name="mla_out",
    )(x, ol, wuv, wo, gate, gpost)


def _prep_weights(p, cdt):
    d = p['w_in_ab'].shape[0]
    n_main = 4 * HG_HEADS * HG_KEY + SSM_INNER + 2 * SSM_INNER
    w_in = p['w_in_ab']
    w_dt = jnp.pad(w_in[:, n_main:], ((0, 0), (0, LANES - SSM_HEADS)))
    exch = lambda a: jnp.concatenate([a, jnp.roll(a[..., -MLA_ROPE:], MLA_ROPE // 2, axis=-1)], axis=-1)
    w_dkv = exch(p['mla_w_dkv'])
    w_uq = exch(p['mla_w_uq'].reshape(MLA_Q_RANK, MLA_HEADS, MLA_NOPE + MLA_ROPE)).transpose(1, 0, 2)
    c = lambda a: a.astype(cdt)
    return dict(
        ffn_wg=c(p['ffn_wg']), ffn_wu=c(p['ffn_wu']), ffn_wd=c(p['ffn_wd']),
        w_in=c(w_in[:, :n_main]), w_dt=c(w_dt), w_out=c(p['w_out_ab']),
        w_dq=c(p['mla_w_dq']), w_dkv=c(w_dkv),
        w_uq=c(w_uq),
        w_uk=c(p['mla_w_uk'].transpose(1, 2, 0)),
        w_uv=c(p['mla_w_uv'].transpose(1, 0, 2)),
        w_o=c(p['mla_w_o']),
    )


def _rope_tables(pos):
    half = MLA_ROPE // 2
    inv = ROPE_THETA ** (-jnp.arange(half, dtype=F32) / half)
    ang = pos[:, None] * inv[None]
    cos, sin = jnp.cos(ang), jnp.sin(ang)
    return jnp.concatenate([cos, cos, -sin, sin], axis=-1)


def _trunk(x, mod_all, pos, hg_s0, ssm_s0, conv_prev, past, p, w, consts, *, bsz, seq_len, cdt, cfg):
    n, d = x.shape
    per_row = seq_len == 1
    vec = lambda a: a.reshape(1, -1)

    def mods(layer, sub):
        m = mod_all[layer].reshape(bsz, 3, 3, d)[:, sub]
        if per_row:
            return tuple(m[:, k].reshape(1, bsz, d) for k in range(3))
        return tuple(m[:, k].reshape(bsz, 1, d) for k in range(3))

    common = dict(per_row=per_row, seq_len=seq_len, cdt=cdt)

    def ffn(x, layer, which, sub):
        return _ffn(x, mods(layer, sub), vec(p['norm_pre'][layer, sub]), vec(p['norm_post'][layer, sub]),
                    w['ffn_wg'][layer, which], w['ffn_wu'][layer, which], w['ffn_wd'][layer, which],
                    tm_pref=cfg['tm_ffn'], fb=cfg['fb'], **common)

    x = ffn(x, 0, 0, 0)
    m1 = mods(0, 1)
    proj, dtraw = _modmm(x, m1, vec(p['norm_pre'][0, 1]), w['w_in'], w['w_dt'],
                         tm_pref=cfg['tm_in'], tn=SSM_INNER, **common)
    lb = jnp.cumsum(jax.nn.softmax(p['hg_lb_logits'].astype(F32), axis=0), axis=0)[0]
    pad16 = lambda a: jnp.pad(a.astype(F32), (0, LANES - SSM_HEADS)).reshape(1, LANES)
    ssm_vecs = (p['ssm_conv_w'], vec(p['ssm_conv_b']), pad16(p['ssm_dt_bias']), pad16(p['ssm_a_log']),
                vec(jnp.repeat(p['ssm_d'].astype(F32), SSM_HEAD_DIM)), vec(p['ssm_norm']))
    ssm_s0t = ssm_s0.transpose(0, 1, 3, 2)
    if per_row:
        nr = _tile(bsz, SUBLANES)
        o_a, hg_s = _hgrn_step(proj, vec(lb), vec(p['hg_norm']), hg_s0, nr=nr)
        y_b, ssm_st = _ssd_step(proj, dtraw, *ssm_vecs, consts['emat'], conv_prev, ssm_s0t, nr=nr)
    else:
        o_a, hg_s = _hgrn(proj, vec(lb), vec(p['hg_norm']), hg_s0, bsz=bsz, l_pad=seq_len, l_true=seq_len,
                          chunk=cfg['hg_chunk'], n_inner=cfg['hg_inner'], hps=cfg['hg_hps'], cdt=cdt)
        cprev8 = jnp.pad(conv_prev, ((0, 0), (SUBLANES - (SSM_CONV - 1), 0), (0, 0)))
        y_b, ssm_st = _ssd(proj, dtraw, *ssm_vecs, consts['emat'], cprev8, ssm_s0t,
                           bsz=bsz, l_pad=seq_len, l_true=seq_len, chunk=cfg['ssd_chunk'], cdt=cdt)
    ssm_s = ssm_st.transpose(0, 1, 3, 2)
    n_tail = min(seq_len, SSM_CONV - 1)
    xbc_tail = proj.reshape(bsz, seq_len, -1)[:, seq_len - n_tail:, 5 * SSM_INNER:7 * SSM_INNER]
    conv_s = jnp.concatenate([conv_prev, xbc_tail], axis=1)[:, -(SSM_CONV - 1):]
    x = _outproj(x, o_a, y_b, w['w_out'], m1[2], vec(p['norm_post'][0, 1]), tm_pref=cfg['tm_mm'], **common)
    x = ffn(x, 0, 1, 2)

    x = ffn(x, 1, 0, 0)
    m1 = mods(1, 1)
    qraw, kvraw = _modmm(x, m1, vec(p['norm_pre'][1, 1]), w['w_dq'], w['w_dkv'],
                         tm_pref=cfg['tm_mm'], tn=MLA_Q_RANK, **common)
    q_scale = 1.0 if past is not None else MLA_SCALE * math.log2(math.e)
    qx, ckv, kpe, kx, kxt = _mlaq(qraw, kvraw, vec(p['mla_g_q']), vec(p['mla_g_kv']), w['w_uq'], w['w_uk'],
                                  pos, cdt=cdt, tm_pref=cfg['tm_mm'], q_scale=q_scale)
    if past is None:
        ol = _attn(qx, kx, kxt, bsz=bsz, seq_len=seq_len, tq=cfg['tq'], tk=cfg['tk'], hc=cfg['hc'])
    else:
        cache_c, cache_p, page_table = past
        o_dec = _decode(page_table, qx.transpose(1, 0, 2), ckv.reshape(bsz, 1, -1), kpe.reshape(bsz, 1, -1),
                        cache_c, cache_p.transpose(0, 2, 1), pg=cfg['pg'])
        ol = o_dec.transpose(1, 0, 2)
    x = _mlaout(x, ol, w['w_uv'], w['w_o'], m1[2], vec(p['norm_post'][1, 1]), tm_pref=cfg['tm_mm'], **common)
    x = ffn(x, 1, 1, 2)
    return x, hg_s, ssm_s, conv_s, ckv, kpe


def kernel(x_prompt, x_sample, c_prompt, c_sample, state_hgrn, state_ssm, state_conv, cache_ckv, cache_kpe, page_table, ada_w, ada_b, norm_pre, norm_post, ffn_wg, ffn_wu, ffn_wd, w_in_ab, w_out_ab, hg_lb_logits, hg_norm, ssm_conv_w, ssm_conv_b, ssm_dt_bias, ssm_a_log, ssm_d, ssm_norm, mla_w_dq, mla_g_q, mla_w_uq, mla_w_dkv, mla_g_kv, mla_w_uk, mla_w_uv, mla_w_o):
    p = dict(norm_pre=norm_pre, norm_post=norm_post, ffn_wg=ffn_wg, ffn_wu=ffn_wu, ffn_wd=ffn_wd,
             w_in_ab=w_in_ab, w_out_ab=w_out_ab, hg_lb_logits=hg_lb_logits, hg_norm=hg_norm,
             ssm_conv_w=ssm_conv_w, ssm_conv_b=ssm_conv_b, ssm_dt_bias=ssm_dt_bias, ssm_a_log=ssm_a_log,
             ssm_d=ssm_d, ssm_norm=ssm_norm, mla_w_dq=mla_w_dq, mla_g_q=mla_g_q, mla_w_uq=mla_w_uq,
             mla_w_dkv=mla_w_dkv, mla_g_kv=mla_g_kv, mla_w_uk=mla_w_uk, mla_w_uv=mla_w_uv, mla_w_o=mla_w_o)
    bp, seq, d = x_prompt.shape
    db, dseq, _ = x_sample.shape
    assert dseq == 1
    n_pages = page_table.shape[1]
    past_len = n_pages * PAGE_SIZE
    f = ffn_wg.shape[-1]

    head_of = np.arange(SSM_INNER) // SSM_HEAD_DIM
    emat = jnp.asarray((np.arange(LANES)[:, None] == head_of[None, :]).astype(np.float32), dtype=BF16)
    consts = dict(emat=emat)

    mod_all = _ada_mod(jnp.concatenate([c_prompt, c_sample], axis=0), ada_w, ada_b)

    fb = 2 * LANES if f % (2 * LANES) == 0 else f
    w_lo = _prep_weights(p, BF16)
    cfg_p = dict(tm_ffn=512, fb=fb, tm_mm=512, tm_in=256, l_pad=seq, hg_chunk=min(64, seq), hg_inner=max(1, min(16, seq // 64)),
                 hg_hps=1,
                 ssd_chunk=min(256, seq), tq=min(128, seq), tk=min(512, seq), hc=2)
    hg0 = jnp.zeros((bp, HG_HEADS, HG_KEY, HG_VAL), F32)
    ssm0 = jnp.zeros((bp, SSM_HEADS, SSM_STATE, SSM_HEAD_DIM), F32)
    conv0 = jnp.zeros((bp, SSM_CONV - 1, 2 * SSM_INNER), F32)
    pos_p = jnp.tile(_rope_tables(jnp.arange(seq, dtype=F32)), (bp, 1))
    y_p, hg_p, ssm_p, conv_p, ckv_p, kpe_p = _trunk(
        x_prompt.reshape(bp * seq, d), mod_all[:, :bp], pos_p, hg0, ssm0, conv0, None, p, w_lo, consts,
        bsz=bp, seq_len=seq, cdt=BF16, cfg=cfg_p)

    w_hi = _prep_weights(p, F32)
    pg = 64 if n_pages % 64 == 0 else n_pages
    cfg_s = dict(tm_ffn=128, fb=fb, tm_mm=128, tm_in=128, l_pad=SUBLANES, hg_chunk=SUBLANES, hg_inner=1, hg_hps=HG_HEADS,
                 ssd_chunk=SUBLANES, pg=pg)
    pos_s = _rope_tables(jnp.full((db,), past_len, F32))
    y_s, hg_s, ssm_s, conv_s, ckv_s, kpe_s = _trunk(
        x_sample.reshape(db, d), mod_all[:, bp:], pos_s, state_hgrn, state_ssm, state_conv,
        (cache_ckv, cache_kpe, page_table), p, w_hi, consts, bsz=db, seq_len=1, cdt=F32, cfg=cfg_s)

    return (y_p.reshape(bp, seq, d), y_s.reshape(db, 1, d), hg_p, hg_s, ssm_p, ssm_s, conv_p, conv_s,
            ckv_p.reshape(bp, seq, -1), ckv_s.reshape(db, 1, -1), kpe_p.reshape(bp, seq, -1),
            kpe_s.reshape(db, 1, -1))
```

```python
import functools
import math

import jax
import jax.numpy as jnp
import numpy as np
from jax import lax
from jax.experimental import pallas as pl
from jax.experimental.pallas import tpu as pltpu

F32 = jnp.float32
BF16 = jnp.bfloat16
HI = lax.Precision.HIGHEST

EPS = 1e-6
MACARON_W = 0.5
ROPE_THETA = 10000.0
PAGE_SIZE = 128

LANES = 128
SUBLANES = 8
VMEM_LIMIT = 56 * 1024 * 1024

HG_HEADS = 8
HG_KEY = 128
HG_VAL = 128
SSM_HEADS = 16
SSM_HEAD_DIM = 64
SSM_GROUPS = 4
SSM_STATE = 128
SSM_CONV = 4
SSM_INNER = SSM_HEADS * SSM_HEAD_DIM
SSM_GROUP_W = SSM_INNER // SSM_GROUPS
HEADS_PER_GROUP = SSM_HEADS // SSM_GROUPS
MLA_HEADS = 16
MLA_Q_RANK = 512
MLA_KV_RANK = 256
MLA_NOPE = 128
MLA_ROPE = 64
MLA_V = 128
MLA_SCALE = (MLA_NOPE + MLA_ROPE) ** -0.5
DECODE_SUB_PAGES = 8
FFN_AHEAD = 2
MLA_Q_AHEAD = 2
ATTN_AHEAD = 2
HG_SUB = SUBLANES


def _cparams(sem):
    return pltpu.CompilerParams(dimension_semantics=sem, vmem_limit_bytes=VMEM_LIMIT)


def _dot(a, b, cdt):
    if cdt == F32:
        return jnp.dot(a.astype(F32), b.astype(F32), precision=HI, preferred_element_type=F32)
    return jnp.dot(a.astype(BF16), b.astype(BF16), preferred_element_type=F32)


def _dot_nt(a, b, cdt):
    dn = (((1,), (1,)), ((), ()))
    if cdt == F32:
        return lax.dot_general(a.astype(F32), b.astype(F32), dn, precision=HI, preferred_element_type=F32)
    return lax.dot_general(a.astype(BF16), b.astype(BF16), dn, preferred_element_type=F32)


def _dot_hi(a, b):
    return jnp.dot(a, b, precision=HI, preferred_element_type=F32)


def _rms(x, g):
    r = lax.rsqrt(jnp.mean(x * x, axis=-1, keepdims=True) + EPS)
    return (x * r) * g


def _sigmoid(x):
    return 1.0 / (1.0 + jnp.exp(-x))


def _silu(x):
    return x * _sigmoid(x)


def _cumsum_rows(x):
    n = x.shape[0]
    row = lax.broadcasted_iota(jnp.int32, (n, 1), 0)
    s = 1
    while s < n:
        x = x + jnp.where(row >= s, pltpu.roll(x, s, 0), 0.0)
        s *= 2
    return x


def _expand_heads(a, e):
    hi = a.astype(BF16)
    r1 = a - hi.astype(F32)
    mid = r1.astype(BF16)
    lo = (r1 - mid.astype(F32)).astype(BF16)
    return (jnp.dot(hi, e, preferred_element_type=F32) + jnp.dot(mid, e, preferred_element_type=F32)
            + jnp.dot(lo, e, preferred_element_type=F32))


def _tile(n, pref):
    t = min(n, pref)
    while n % t:
        t -= SUBLANES
    return t


def _mod_spec(per_row, tm, tiles_per_seq, d):
    if per_row:
        return pl.BlockSpec((1, tm, d), lambda i, *_: (0, i, 0))
    return pl.BlockSpec((1, 1, d), lambda i, *_: (i // tiles_per_seq, 0, 0))


def _ada_kernel(c_ref, w_ref, b_ref, o_ref):
    cs = _silu(c_ref[...])
    o_ref[0] = _dot_hi(cs, w_ref[0]) + b_ref[0]


def _ada_mod(c_all, ada_w, ada_b):
    depth, d, n = ada_w.shape
    m = c_all.shape[0]
    tn = _tile(n, 1152)
    return pl.pallas_call(
        _ada_kernel,
        grid=(depth, n // tn),
        in_specs=[pl.BlockSpec((m, d), lambda l, j: (0, 0)),
                  pl.BlockSpec((1, d, tn), lambda l, j: (l, 0, j)),
                  pl.BlockSpec((1, 1, tn), lambda l, j: (l, 0, j))],
        out_specs=pl.BlockSpec((1, m, tn), lambda l, j: (l, 0, j)),
        out_shape=jax.ShapeDtypeStruct((depth, m, n), F32),
        compiler_params=_cparams(("parallel", "parallel")),
        name="ada_mod",
    )(c_all, ada_w, ada_b.reshape(depth, 1, n))


def _ffn_kernel(x_ref, gpre_ref, sh_ref, sc_ref, gt_ref, gpost_ref, wg_ref, wu_ref, wd_ref,
                o_ref, a_scr, *, cdt, fb):
    x = x_ref[...]
    h = (_rms(x, gpre_ref[...]) * (1.0 + sc_ref[0]) + sh_ref[0]).astype(cdt)
    nb = wg_ref.shape[1] // fb

    def gate_up(s):
        fs = slice(s * fb, (s + 1) * fb)
        return _dot(h, wg_ref[:, fs], cdt), _dot(h, wu_ref[:, fs], cdt)

    ahead = [gate_up(s) for s in range(min(FFN_AHEAD, nb))]
    for s in range(nb):
        g, u = ahead.pop(0)
        if s + FFN_AHEAD < nb:
            ahead.append(gate_up(s + FFN_AHEAD))
        a_scr[:, s * fb:(s + 1) * fb] = (_silu(g) * u).astype(cdt)
    y = _dot(a_scr[...], wd_ref[...], cdt)
    o_ref[...] = x + MACARON_W * gt_ref[0] * _rms(y, gpost_ref[...])


def _ffn(x, mods, gpre, gpost, wg, wu, wd, *, per_row, seq_len, cdt, tm_pref, fb):
    n, d = x.shape
    f = wg.shape[1]
    tm = _tile(seq_len if not per_row else n, tm_pref)
    tps = max(seq_len // tm, 1)
    shift, scale, gate = mods
    mspec = _mod_spec(per_row, tm, tps, d)
    vec = pl.BlockSpec((1, d), lambda i: (0, 0))
    whole = lambda a: pl.BlockSpec(a.shape, lambda i: (0, 0), pipeline_mode=pl.Buffered(1))
    return pl.pallas_call(
        functools.partial(_ffn_kernel, cdt=cdt, fb=fb),
        grid=(n // tm,),
        in_specs=[pl.BlockSpec((tm, d), lambda i: (i, 0)), vec, mspec, mspec, mspec, vec,
                  whole(wg), whole(wu), whole(wd)],
        out_specs=pl.BlockSpec((tm, d), lambda i: (i, 0)),
        out_shape=jax.ShapeDtypeStruct((n, d), F32),
        scratch_shapes=[pltpu.VMEM((tm, f), cdt)],
        compiler_params=_cparams(("parallel",)),
        name="ffn",
    )(x, gpre, shift, scale, gate, gpost, wg, wu, wd)


def _modmm_kernel(x_ref, gpre_ref, sh_ref, sc_ref, w_ref, wx_ref, o_ref, ox_ref, *, cdt, tn):
    h = (_rms(x_ref[...], gpre_ref[...]) * (1.0 + sc_ref[0]) + sh_ref[0]).astype(cdt)
    ox_ref[...] = _dot(h, wx_ref[...], cdt)
    for jb in range(w_ref.shape[1] // tn):
        cs = slice(jb * tn, (jb + 1) * tn)
        o_ref[:, cs] = _dot(h, w_ref[:, cs], cdt)


def _modmm(x, mods, gpre, w, wx, *, per_row, seq_len, cdt, tm_pref, tn):
    n, d = x.shape
    nout = w.shape[1]
    nx = wx.shape[1]
    tm = _tile(seq_len if not per_row else n, tm_pref)
    tps = max(seq_len // tm, 1)
    shift, scale, _ = mods
    mspec = _mod_spec(per_row, tm, tps, d)
    whole = lambda a: pl.BlockSpec(a.shape, lambda i: (0, 0), pipeline_mode=pl.Buffered(1))
    return pl.pallas_call(
        functools.partial(_modmm_kernel, cdt=cdt, tn=tn),
        grid=(n // tm,),
        in_specs=[pl.BlockSpec((tm, d), lambda i: (i, 0)),
                  pl.BlockSpec((1, d), lambda i: (0, 0)), mspec, mspec, whole(w), whole(wx)],
        out_specs=[pl.BlockSpec((tm, nout), lambda i: (i, 0)),
                   pl.BlockSpec((tm, nx), lambda i: (i, 0))],
        out_shape=[jax.ShapeDtypeStruct((n, nout), F32), jax.ShapeDtypeStruct((n, nx), F32)],
        compiler_params=_cparams(("parallel",)),
        name="modmm",
    )(x, gpre, shift, scale, w, wx)


def _hgrn_kernel(q_ref, f_ref, v_ref, g_ref, lb_ref, gn_ref, s0_ref, o_ref, sout_ref, st_scr,
                 *, chunk, n_inner, hps, l_true, nl, cdt):
    l = pl.program_id(2)
    t_blk = chunk * n_inner
    sub = HG_SUB
    nsub = chunk // sub

    @pl.when(l == 0)
    def _():
        for hh in range(hps):
            st_scr[hh] = s0_ref[0, hh].T

    rowc = lax.broadcasted_iota(jnp.int32, (chunk, 1), 0)
    tril = (lax.broadcasted_iota(jnp.int32, (chunk, chunk), 0)
            >= lax.broadcasted_iota(jnp.int32, (chunk, chunk), 1)).astype(F32)
    srow = lax.broadcasted_iota(jnp.int32, (nsub, sub, 1), 1)
    blocks = lambda a: a.reshape(nsub, sub, a.shape[-1])

    for hh in range(hps):
        cols = slice(hh * HG_KEY, (hh + 1) * HG_KEY)
        lbh = lb_ref[:, cols]
        gn = gn_ref[:, cols]
        pre = []
        for c in range(n_inner):
            rows = slice(c * chunk, (c + 1) * chunk)
            valid = (l * t_blk + c * chunk + rowc) < l_true
            fr = f_ref[rows, cols]
            v = v_ref[rows, cols]
            f = lbh + (1.0 - lbh) * _sigmoid(fr)
            logf = jnp.where(valid, jnp.log(f), 0.0)
            k = jnp.where(valid, (1.0 - lbh) * _sigmoid(-fr), 0.0)
            q = _silu(q_ref[rows, cols])
            b = _dot_hi(tril, logf)
            pre.append((rows, q, k, v, b))
        atts = []
        for rows, q, k, v, b in pre:
            att_c = []
            for i in range(1, nsub):
                lo = i * sub
                hi = min(chunk, -(-lo // 16) * 16)
                ref_b = b[lo - 1:lo, :]
                qi = q[lo:lo + sub, :] * jnp.exp(b[lo:lo + sub, :] - ref_b)
                ki = jnp.where(rowc[0:hi] < lo, k[0:hi, :] * jnp.exp(ref_b - b[0:hi, :]), 0.0)
                att_c.append((_dot_nt(qi, ki, cdt), hi))
            atts.append(att_c)
        intra, incs = [], []
        for (rows, q, k, v, b), att_c in zip(pre, atts):
            q3, k3, b3, v3 = blocks(q), blocks(k), blocks(b), blocks(v)
            o3 = jnp.sum(q3 * k3, axis=-1, keepdims=True) * v3
            for d in range(1, sub):
                k_sh = pltpu.roll(k3, d, 1)
                b_sh = pltpu.roll(b3, d, 1)
                v_sh = pltpu.roll(v3, d, 1)
                a = jnp.sum(q3 * k_sh * jnp.exp(b3 - b_sh), axis=-1, keepdims=True)
                o3 = o3 + jnp.where(srow >= d, a, 0.0) * v_sh
            o = o3.reshape(chunk, HG_VAL)
            if nsub > 1:
                parts = [jnp.zeros((sub, HG_VAL), F32)] + [_dot(att, v[0:hi, :], cdt) for att, hi in att_c]
                o = o + jnp.concatenate(parts, axis=0)
            intra.append(o)
            bl = b[chunk - 1:chunk, :]
            incs.append(_dot(v.T, k * jnp.exp(bl - b), cdt))
        st = st_scr[hh]
        for (rows, q, k, v, b), o, inc in zip(pre, intra, incs):
            o = o + _dot_nt(q * jnp.exp(b), st, cdt)
            st = st * jnp.exp(b[chunk - 1:chunk, :]) + inc
            o_ref[rows, cols] = _rms(o, gn) * _silu(g_ref[rows, cols])
        st_scr[hh] = st

    @pl.when(l == nl - 1)
    def _():
        for hh in range(hps):
            sout_ref[0, hh] = st_scr[hh].T


def _hgrn(proj, lb, gn, s0, *, bsz, l_pad, l_true, chunk, n_inner, hps, cdt):
    t_blk = chunk * n_inner
    nl = l_pad // t_blk
    n = bsz * l_pad
    nhb = HG_HEADS // hps
    wide = hps * HG_KEY

    def col(cb):
        return pl.BlockSpec((t_blk, wide), lambda b, h, l: (b * nl + l, cb * nhb + h))

    hvec = pl.BlockSpec((1, wide), lambda b, h, l: (0, h))
    sspec = pl.BlockSpec((1, hps, HG_KEY, HG_VAL), lambda b, h, l: (b, h, 0, 0))
    return pl.pallas_call(
        functools.partial(_hgrn_kernel, chunk=chunk, n_inner=n_inner, hps=hps, l_true=l_true, nl=nl, cdt=cdt),
        grid=(bsz, nhb, nl),
        in_specs=[col(0), col(1), col(2), col(3), hvec, hvec, sspec],
        out_specs=[pl.BlockSpec((t_blk, wide), lambda b, h, l: (b * nl + l, h)), sspec],
        out_shape=[jax.ShapeDtypeStruct((n, HG_HEADS * HG_VAL), F32),
                   jax.ShapeDtypeStruct(s0.shape, F32)],
        scratch_shapes=[pltpu.VMEM((hps, HG_VAL, HG_KEY), F32)],
        compiler_params=_cparams(("parallel", "parallel", "arbitrary")),
        name="hgrn2",
    )(proj, proj, proj, proj, lb, gn, s0)


def _hgrn_step_kernel(q_ref, f_ref, v_ref, g_ref, lb_ref, gn_ref, s0_ref, o_ref, sout_ref):
    nr = q_ref.shape[0]
    for h in range(HG_HEADS):
        cols = slice(h * HG_KEY, (h + 1) * HG_KEY)
        lbh = lb_ref[:, cols]
        fr = f_ref[:, cols]
        f_t = (lbh + (1.0 - lbh) * _sigmoid(fr)).T
        k_t = ((1.0 - lbh) * _sigmoid(-fr)).T
        q_t = _silu(q_ref[:, cols]).T
        v = v_ref[:, cols]
        rows = []
        for r in range(nr):
            s_new = f_t[:, r:r + 1] * s0_ref[r, h] + k_t[:, r:r + 1] * v[r:r + 1, :]
            sout_ref[r, h] = s_new
            rows.append(jnp.sum(q_t[:, r:r + 1] * s_new, axis=0, keepdims=True))
        o = jnp.concatenate(rows, axis=0)
        o_ref[:, cols] = _rms(o, gn_ref[:, cols]) * _silu(g_ref[:, cols])


def _hgrn_step(proj, lb, gn, s0, *, nr):
    n = proj.shape[0]
    wide = HG_HEADS * HG_KEY
    col = lambda cb: pl.BlockSpec((nr, wide), lambda i: (i, cb))
    vecs = pl.BlockSpec((1, wide), lambda i: (0, 0))
    sspec = pl.BlockSpec((nr, HG_HEADS, HG_KEY, HG_VAL), lambda i: (i, 0, 0, 0))
    return pl.pallas_call(
        _hgrn_step_kernel,
        grid=(n // nr,),
        in_specs=[col(0), col(1), col(2), col(3), vecs, vecs, sspec],
        out_specs=[pl.BlockSpec((nr, wide), lambda i: (i, 0)), sspec],
        out_shape=[jax.ShapeDtypeStruct((n, wide), F32), jax.ShapeDtypeStruct(s0.shape, F32)],
        compiler_params=_cparams(("parallel",)),
        name="hgrn2_step",
    )(proj, proj, proj, proj, lb, gn, s0)


def _ssd_step_kernel(z_ref, xa_ref, xb_ref, dt_ref, cw_ref, cb_ref, dtb_ref, alog_ref, dexp_ref, nrm_ref,
                     e_ref, cprev_ref, s0_ref, y_ref, sout_ref):
    nr = z_ref.shape[0]
    gw = SSM_GROUP_W
    xraw = jnp.concatenate([xa_ref[...], xb_ref[...]], axis=-1)
    conv = cb_ref[...] + cw_ref[SSM_CONV - 1:SSM_CONV, :] * xraw
    for i in range(SSM_CONV - 1):
        conv = conv + cw_ref[i:i + 1, :] * cprev_ref[:, i, :]
    xc = _silu(conv)
    xh = xc[:, 0:SSM_INNER]
    bm = xc[:, SSM_INNER:SSM_INNER + SSM_GROUPS * SSM_STATE]
    cm = xc[:, SSM_INNER + SSM_GROUPS * SSM_STATE:]
    draw = dt_ref[...] + dtb_ref[...]
    dt = jnp.maximum(draw, 0.0) + jnp.log1p(jnp.exp(-jnp.abs(draw)))
    e = e_ref[...]
    dec_t = _expand_heads(jnp.exp(dt * (-jnp.exp(alog_ref[...]))), e).T
    xd_t = (xh * _expand_heads(dt, e)).T
    for h in range(SSM_HEADS):
        g = h // HEADS_PER_GROUP
        hs = slice(h * SSM_HEAD_DIM, (h + 1) * SSM_HEAD_DIM)
        bg = bm[:, g * SSM_STATE:(g + 1) * SSM_STATE]
        for r in range(nr):
            sout_ref[r, h] = dec_t[hs, r:r + 1] * s0_ref[r, h] + xd_t[hs, r:r + 1] * bg[r:r + 1, :]
    y_parts = []
    for h in range(SSM_HEADS):
        g = h // HEADS_PER_GROUP
        cg = cm[:, g * SSM_STATE:(g + 1) * SSM_STATE]
        cols = [jnp.sum(sout_ref[r, h] * cg[r:r + 1, :], axis=-1, keepdims=True) for r in range(nr)]
        y_parts.append(jnp.concatenate(cols, axis=-1))
    y = jnp.concatenate(y_parts, axis=0).T
    y = (y + dexp_ref[...] * xh) * _silu(z_ref[...])
    outs = []
    for g in range(SSM_GROUPS):
        gs = slice(g * gw, (g + 1) * gw)
        outs.append(_rms(y[:, gs], nrm_ref[:, gs]))
    y_ref[...] = jnp.concatenate(outs, axis=-1)


def _ssd_step(proj, dtraw, cw, cb, dtb, alog, dexp, nrm, emat, cprev, s0t, *, nr):
    n = proj.shape[0]
    w = SSM_INNER
    zcol = 4 * HG_HEADS * HG_KEY // w
    cc = 2 * w
    col = lambda cbk: pl.BlockSpec((nr, w), lambda i: (i, cbk))
    const = lambda shape: pl.BlockSpec(shape, lambda i: (0,) * len(shape))
    sspec = pl.BlockSpec((nr, SSM_HEADS, SSM_HEAD_DIM, SSM_STATE), lambda i: (i, 0, 0, 0))
    return pl.pallas_call(
        _ssd_step_kernel,
        grid=(n // nr,),
        in_specs=[col(zcol), col(zcol + 1), col(zcol + 2), pl.BlockSpec((nr, LANES), lambda i: (i, 0)),
                  const((SSM_CONV, cc)), const((1, cc)), const((1, LANES)), const((1, LANES)),
                  const((1, w)), const((1, w)), const((LANES, w)),
                  pl.BlockSpec((nr, SSM_CONV - 1, cc), lambda i: (i, 0, 0)), sspec],
        out_specs=[pl.BlockSpec((nr, w), lambda i: (i, 0)), sspec],
        out_shape=[jax.ShapeDtypeStruct((n, w), F32), jax.ShapeDtypeStruct(s0t.shape, F32)],
        compiler_params=_cparams(("parallel",)),
        name="ssd_step",
    )(proj, proj, proj, dtraw, cw, cb, dtb, alog, dexp, nrm, emat, cprev, s0t)


def _ssd_kernel(z_ref, xa_ref, xb_ref, dt_ref, cw_ref, cb_ref, dtb_ref, alog_ref, dexp_ref, nrm_ref,
                e_ref, cprev_ref, s0_ref, y_ref, sout_ref, xpad_scr, s_scr, *, chunk, l_true, nl, cdt):
    l = pl.program_id(1)
    c = chunk
    gw = SSM_GROUP_W

    @pl.when(l == 0)
    def _():
        xpad_scr[0:SUBLANES, :] = cprev_ref[0]
        for g in range(SSM_GROUPS):
            s_scr[g] = s0_ref[0, g * HEADS_PER_GROUP:(g + 1) * HEADS_PER_GROUP].reshape(gw, SSM_STATE)

    @pl.when(l > 0)
    def _():
        xpad_scr[0:SUBLANES, :] = xpad_scr[c:c + SUBLANES, :]

    xpad_scr[SUBLANES:SUBLANES + c, 0:SSM_INNER] = xa_ref[...]
    xpad_scr[SUBLANES:SUBLANES + c, SSM_INNER:2 * SSM_INNER] = xb_ref[...]
    conv = cb_ref[...]
    for i in range(SSM_CONV):
        lo = SUBLANES - (SSM_CONV - 1) + i
        conv = conv + cw_ref[i:i + 1, :] * xpad_scr[lo:lo + c, :]
    xc = _silu(conv)
    xh = xc[:, 0:SSM_INNER]
    bm = xc[:, SSM_INNER:SSM_INNER + SSM_GROUPS * SSM_STATE]
    cm = xc[:, SSM_INNER + SSM_GROUPS * SSM_STATE:]

    rowc = lax.broadcasted_iota(jnp.int32, (c, 1), 0)
    valid = (l * c + rowc) < l_true
    draw = dt_ref[...] + dtb_ref[...]
    dt = jnp.maximum(draw, 0.0) + jnp.log1p(jnp.exp(-jnp.abs(draw)))
    dt = jnp.where(valid, dt, 0.0)
    la = dt * (-jnp.exp(alog_ref[...]))
    ri = lax.broadcasted_iota(jnp.int32, (c, c), 0)
    ci = lax.broadcasted_iota(jnp.int32, (c, c), 1)
    causal = ri >= ci
    b = _cumsum_rows(la)
    b_t = b.T
    dt_t = dt.T
    bl = b[c - 1:c, :]
    expand = functools.partial(_expand_heads, e=e_ref[...])

    lane = lax.broadcasted_iota(jnp.int32, (1, gw), 1)
    grp = lambda a, g: a[:, g * SSM_STATE:(g + 1) * SSM_STATE]
    gsl = lambda g: slice(g * gw, (g + 1) * gw)
    cbs = [_dot_nt(grp(cm, g), grp(bm, g), cdt) for g in range(SSM_GROUPS)]
    carried = [_dot_nt(grp(cm, g), s_scr[g], cdt) for g in range(SSM_GROUPS)]
    eb_e = expand(jnp.exp(b))
    dtw_e = expand(dt * jnp.exp(bl - b))
    for g in range(SSM_GROUPS):
        gs = gsl(g)
        decay = jnp.broadcast_to(eb_e[c - 1:c, gs], (SUBLANES, gw)).T[:, 0:1]
        s_scr[g] = decay * s_scr[g] + _dot((xh[:, gs] * dtw_e[:, gs]).T, grp(bm, g), cdt)
    ys = []
    for g in range(SSM_GROUPS):
        gs = gsl(g)
        yg = carried[g] * eb_e[:, gs]
        for j in range(HEADS_PER_GROUP):
            h = g * HEADS_PER_GROUP + j
            seg = jnp.exp(jnp.where(causal, b[:, h:h + 1] - b_t[h:h + 1, :], -jnp.inf)) * dt_t[h:h + 1, :]
            in_head = (lane >= j * SSM_HEAD_DIM) & (lane < (j + 1) * SSM_HEAD_DIM)
            yg = yg + _dot(cbs[g] * seg, jnp.where(in_head, xh[:, gs], 0.0), cdt)
        ys.append(yg)
    y = jnp.concatenate(ys, axis=-1)
    y = (y + dexp_ref[...] * xh) * _silu(z_ref[...])
    outs = []
    for g in range(SSM_GROUPS):
        gs = slice(g * gw, (g + 1) * gw)
        outs.append(_rms(y[:, gs], nrm_ref[:, gs]))
    y_ref[...] = jnp.concatenate(outs, axis=-1)

    @pl.when(l == nl - 1)
    def _():
        for g in range(SSM_GROUPS):
            sout_ref[0, g * HEADS_PER_GROUP:(g + 1) * HEADS_PER_GROUP] = s_scr[g].reshape(
                HEADS_PER_GROUP, SSM_HEAD_DIM, SSM_STATE)


def _ssd(proj, dtraw, cw, cb, dtb, alog, dexp, nrm, emat, cprev, s0g, *, bsz, l_pad, l_true, chunk, cdt):
    nl = l_pad // chunk
    n = bsz * l_pad
    w = SSM_INNER
    zcol = 4 * HG_HEADS * HG_KEY // w
    cc = 2 * w

    def col(cbk):
        return pl.BlockSpec((chunk, w), lambda b, l: (b * nl + l, cbk))

    def const(shape):
        return pl.BlockSpec(shape, lambda b, l: (0,) * len(shape))

    sspec = pl.BlockSpec((1, SSM_HEADS, SSM_HEAD_DIM, SSM_STATE), lambda b, l: (b, 0, 0, 0))
    return pl.pallas_call(
        functools.partial(_ssd_kernel, chunk=chunk, l_true=l_true, nl=nl, cdt=cdt),
        grid=(bsz, nl),
        in_specs=[col(zcol), col(zcol + 1), col(zcol + 2),
                  pl.BlockSpec((chunk, LANES), lambda b, l: (b * nl + l, 0)),
                  const((SSM_CONV, cc)), const((1, cc)), const((1, LANES)), const((1, LANES)),
                  const((1, w)), const((1, w)), const((LANES, w)),
                  pl.BlockSpec((1, SUBLANES, cc), lambda b, l: (b, 0, 0)), sspec],
        out_specs=[pl.BlockSpec((chunk, w), lambda b, l: (b * nl + l, 0)), sspec],
        out_shape=[jax.ShapeDtypeStruct((n, w), F32), jax.ShapeDtypeStruct(s0g.shape, F32)],
        scratch_shapes=[pltpu.VMEM((chunk + 2 * SUBLANES, cc), F32),
                        pltpu.VMEM((SSM_GROUPS, SSM_GROUP_W, SSM_STATE), F32)],
        compiler_params=_cparams(("parallel", "arbitrary")),
        name="ssd",
    )(proj, proj, proj, dtraw, cw, cb, dtb, alog, dexp, nrm, emat, cprev, s0g)


def _outproj_kernel(x_ref, a_ref, b_ref, w_ref, gt_ref, gpost_ref, o_ref, *, cdt):
    ka = a_ref.shape[1]
    y = _dot(a_ref[...], w_ref[0:ka, :], cdt) + _dot(b_ref[...], w_ref[ka:, :], cdt)
    o_ref[...] = x_ref[...] + gt_ref[0] * _rms(y, gpost_ref[...])


def _outproj(x, a, b, w, gate, gpost, *, per_row, seq_len, cdt, tm_pref):
    n, d = x.shape
    tm = _tile(seq_len if not per_row else n, tm_pref)
    tps = max(seq_len // tm, 1)
    mspec = _mod_spec(per_row, tm, tps, d)
    row = lambda width: pl.BlockSpec((tm, width), lambda i: (i, 0))
    return pl.pallas_call(
        functools.partial(_outproj_kernel, cdt=cdt),
        grid=(n // tm,),
        in_specs=[row(d), row(a.shape[1]), row(b.shape[1]),
                  pl.BlockSpec(w.shape, lambda i: (0, 0)), mspec,
                  pl.BlockSpec((1, d), lambda i: (0, 0))],
        out_specs=row(d),
        out_shape=jax.ShapeDtypeStruct((n, d), F32),
        compiler_params=_cparams(("parallel",)),
        name="outproj",
    )(x, a, b, w, gate, gpost)


def _rope(xs, cs_t):
    u = xs * cs_t
    return u + pltpu.roll(u, MLA_ROPE, 1)


def _mlaq_kernel(qr_ref, kv_ref, gq_ref, gkv_ref, wuq_ref, wuk_ref, cs_ref,
                 qx_ref, ckv_ref, kpe_ref, kx_ref, kxt_ref, *, cdt, q_scale):
    cs_t = cs_ref[...]
    lane = lax.broadcasted_iota(jnp.int32, (1, 2 * MLA_ROPE), 1)
    low = lane < MLA_ROPE
    cq = _rms(qr_ref[...], gq_ref[...]).astype(cdt)
    kv = kv_ref[...]
    ckv = _rms(kv[:, 0:MLA_KV_RANK], gkv_ref[...])
    kpe = _rope(kv[:, MLA_KV_RANK:MLA_KV_RANK + 2 * MLA_ROPE], cs_t)
    ckv_ref[...] = ckv
    kpe_ref[...] = kpe[:, 0:MLA_ROPE]
    kx = jnp.concatenate([ckv, jnp.where(low, kpe, 0.0)], axis=-1)
    kx_ref[...] = kx.astype(kx_ref.dtype)
    kxt_ref[...] = kx.T.astype(kxt_ref.dtype)
    up = lambda h: _dot(cq, wuq_ref[h], cdt)
    ahead = [up(h) for h in range(MLA_Q_AHEAD)]
    for h in range(MLA_HEADS):
        q = ahead.pop(0)
        if h + MLA_Q_AHEAD < MLA_HEADS:
            ahead.append(up(h + MLA_Q_AHEAD))
        ql = _dot(q[:, 0:MLA_NOPE], wuk_ref[h], cdt)
        qp = _rope(q[:, MLA_NOPE:MLA_NOPE + 2 * MLA_ROPE], cs_t)
        qx_ref[h, :, 0:MLA_KV_RANK] = (ql * q_scale).astype(qx_ref.dtype)
        qx_ref[h, :, MLA_KV_RANK:] = jnp.where(low, qp * q_scale, 0.0).astype(qx_ref.dtype)


def _mlaq(qraw, kvraw, gq, gkv, wuq, wuk, cs_t, *, cdt, tm_pref, q_scale):
    n = qraw.shape[0]
    tm = _tile(n, tm_pref)
    h = MLA_HEADS
    kw = MLA_KV_RANK + 2 * MLA_ROPE
    row = lambda width: pl.BlockSpec((tm, width), lambda i: (i, 0))
    const = lambda shape: pl.BlockSpec(shape, lambda i: (0,) * len(shape))
    return pl.pallas_call(
        functools.partial(_mlaq_kernel, cdt=cdt, q_scale=q_scale),
        grid=(n // tm,),
        in_specs=[row(MLA_Q_RANK), row(kvraw.shape[1]), const((1, MLA_Q_RANK)), const((1, MLA_KV_RANK)),
                  const(wuq.shape), const(wuk.shape), row(2 * MLA_ROPE)],
        out_specs=[pl.BlockSpec((h, tm, kw), lambda i: (0, i, 0)), row(MLA_KV_RANK), row(MLA_ROPE), row(kw),
                   pl.BlockSpec((kw, tm), lambda i: (0, i))],
        out_shape=[jax.ShapeDtypeStruct((h, n, kw), cdt),
                   jax.ShapeDtypeStruct((n, MLA_KV_RANK), F32),
                   jax.ShapeDtypeStruct((n, MLA_ROPE), F32),
                   jax.ShapeDtypeStruct((n, kw), BF16),
                   jax.ShapeDtypeStruct((kw, n), BF16)],
        compiler_params=_cparams(("parallel",)),
        name="mla_q",
    )(qraw, kvraw, gq, gkv, wuq, wuk, cs_t)


def _attn_kernel(it_ref, jt_ref, qx_ref, kx_ref, kxt_ref, o_ref, m_scr, l_scr, acc_scr, *, tq, tk, hc):
    step = pl.program_id(1)
    i = it_ref[step]
    j = jt_ref[step]
    last = (i * tq + tq - 1) // tk

    @pl.when(j == 0)
    def _():
        m_scr[...] = jnp.full_like(m_scr, -jnp.inf)
        l_scr[...] = jnp.zeros_like(l_scr)
        acc_scr[...] = jnp.zeros_like(acc_scr)

    rc = hc * tq
    kw = qx_ref.shape[2]

    def key_tile(masked, width):
        kc = kx_ref[0:width, 0:MLA_KV_RANK]
        if masked:
            qpos = i * tq + (lax.broadcasted_iota(jnp.int32, (rc, width), 0) & (tq - 1))
            kpos = j * tk + lax.broadcasted_iota(jnp.int32, (rc, width), 1)
            keep = kpos <= qpos
        logits = lambda c: _dot(qx_ref[c * hc:(c + 1) * hc].reshape(rc, kw), kxt_ref[:, 0:width], BF16)
        nblk = MLA_HEADS // hc
        ahead = [logits(c) for c in range(min(ATTN_AHEAD, nblk))]
        for c in range(nblk):
            rs = slice(c * rc, (c + 1) * rc)
            s = ahead.pop(0)
            if c + ATTN_AHEAD < nblk:
                ahead.append(logits(c + ATTN_AHEAD))
            if masked:
                s = jnp.where(keep, s, -jnp.inf)
            m_prev = m_scr[rs]
            m_new = jnp.maximum(m_prev, jnp.max(s, axis=-1, keepdims=True))
            alpha = jnp.exp2(m_prev - m_new)
            p = jnp.exp2(s - jnp.concatenate([m_new] * (width // LANES), axis=-1))
            l_scr[rs] = alpha * l_scr[rs] + jnp.sum(p, axis=-1, keepdims=True)
            acc_scr[rs] = jnp.concatenate([alpha] * 2, axis=-1) * acc_scr[rs] + _dot(p, kc, BF16)
            m_scr[rs] = m_new

    @pl.when(j < last)
    def _():
        key_tile(False, tk)

    @pl.when(j == last)
    def _():
        for v in range(tk // tq):
            @pl.when(i * tq - j * tk == v * tq)
            def _():
                key_tile(True, (v + 1) * tq)
        o = acc_scr[...] / jnp.concatenate([l_scr[...]] * 2, axis=-1)
        o_ref[...] = o.reshape(MLA_HEADS, tq, MLA_KV_RANK).astype(o_ref.dtype)


def _attn(qx, kx, kxt, *, bsz, seq_len, tq, tk, hc):
    h, n, kw = qx.shape
    r = MLA_KV_RANK
    nq = seq_len // tq
    nk = seq_len // tk
    rows = h * tq
    pairs = [(i, j) for i in range(nq) for j in range((i * tq + tq - 1) // tk + 1)]
    i_of = jnp.asarray([ij[0] for ij in pairs], jnp.int32)
    j_of = jnp.asarray([ij[1] for ij in pairs], jnp.int32)
    grid_spec = pltpu.PrefetchScalarGridSpec(
        num_scalar_prefetch=2,
        grid=(bsz, len(pairs)),
        in_specs=[pl.BlockSpec((h, tq, kw), lambda b, s, it, jt: (0, b * nq + it[s], 0)),
                  pl.BlockSpec((tk, kw), lambda b, s, it, jt: (b * nk + jt[s], 0)),
                  pl.BlockSpec((kw, tk), lambda b, s, it, jt: (0, b * nk + jt[s]))],
        out_specs=pl.BlockSpec((h, tq, r), lambda b, s, it, jt: (0, b * nq + it[s], 0)),
        scratch_shapes=[pltpu.VMEM((rows, LANES), F32), pltpu.VMEM((rows, LANES), F32), pltpu.VMEM((rows, r), F32)],
    )
    return pl.pallas_call(
        functools.partial(_attn_kernel, tq=tq, tk=tk, hc=hc),
        grid_spec=grid_spec,
        out_shape=jax.ShapeDtypeStruct((h, n, r), BF16),
        compiler_params=_cparams(("parallel", "arbitrary")),
        name="mla_attn",
    )(i_of, j_of, qx, kx, kxt)


def _decode_kernel(pt_ref, qx_ref, ckn_ref, kpn_ref, cache_c, cache_p, o_ref, cbuf, pbuf, sems,
                   *, n_pages, pg, sub_pg, n_seq):
    bidx = pl.program_id(0)
    n_chunks = n_pages // pg
    total_chunks = n_seq * n_chunks
    sub = sub_pg * PAGE_SIZE

    def page_copies(page, slot, p):
        return (pltpu.make_async_copy(cache_c.at[page], cbuf.at[slot, p], sems.at[0, slot]),
                pltpu.make_async_copy(cache_p.at[page], pbuf.at[slot, p], sems.at[1, slot]))

    def start(g, slot):
        for p in range(pg):
            for cp in page_copies(pt_ref[g * pg + p], slot, p):
                cp.start()

    def wait(slot):
        for p in range(pg):
            for cp in page_copies(0, slot, p):
                cp.wait()

    qx = qx_ref[0]
    ql = qx[:, 0:MLA_KV_RANK]
    qp = qx[:, MLA_KV_RANK:MLA_KV_RANK + MLA_ROPE]

    @pl.when(bidx == 0)
    def _():
        start(0, 0)

    def body(cidx, carry):
        m_prev, l_prev, acc = carry
        g = bidx * n_chunks + cidx
        slot = g % 2

        start(jnp.minimum(g + 1, total_chunks - 1), 1 - slot)
        wait(slot)
        kcs, s_parts = [], []
        for sb in range(pg // sub_pg):
            kc = cbuf[slot, sb * sub_pg:(sb + 1) * sub_pg].reshape(sub, MLA_KV_RANK).astype(BF16)
            kcs.append(kc)
            s_rope = jnp.concatenate(
                [_dot(qp, pbuf[slot, sb * sub_pg + p], BF16) for p in range(sub_pg)], axis=-1)
            s_parts.append(_dot_nt(ql, kc, BF16) + s_rope)
        s = jnp.concatenate(s_parts, axis=-1) * MLA_SCALE
        m_new = jnp.maximum(m_prev, jnp.max(s, axis=-1, keepdims=True))
        alpha = jnp.exp(m_prev - m_new)
        p = jnp.exp(s - m_new)
        l_new = alpha * l_prev + jnp.sum(p, axis=-1, keepdims=True)
        pv = _dot(p[:, 0:sub], kcs[0], BF16)
        for sb in range(1, pg // sub_pg):
            pv = pv + _dot(p[:, sb * sub:(sb + 1) * sub], kcs[sb], BF16)
        return m_new, l_new, alpha * acc + pv

    init = (jnp.full((MLA_HEADS, 1), -jnp.inf, F32), jnp.zeros((MLA_HEADS, 1), F32),
            jnp.zeros((MLA_HEADS, MLA_KV_RANK), F32))
    m_prev, l_prev, acc = lax.fori_loop(0, n_chunks, body, init)

    @pl.when(bidx == n_seq - 1)
    def _():
        wait(total_chunks % 2)

    ckn = ckn_ref[0]
    s_own = (jnp.sum(ql * ckn, axis=-1, keepdims=True)
             + jnp.sum(qp * kpn_ref[0], axis=-1, keepdims=True)) * MLA_SCALE
    m_new = jnp.maximum(m_prev, s_own)
    alpha = jnp.exp(m_prev - m_new)
    p_own = jnp.exp(s_own - m_new)
    l_new = alpha * l_prev + p_own
    o_ref[0] = (alpha * acc + p_own * ckn) / l_new


def _decode(page_table, qx, ckn, kpn, cache_c, cache_pt, *, pg):
    db, n_pages = page_table.shape
    h, r, dr = MLA_HEADS, MLA_KV_RANK, MLA_ROPE
    grid_spec = pltpu.PrefetchScalarGridSpec(
        num_scalar_prefetch=1,
        grid=(db,),
        in_specs=[pl.BlockSpec((1, h, qx.shape[2]), lambda b, pt: (b, 0, 0)),
                  pl.BlockSpec((1, 1, r), lambda b, pt: (b, 0, 0)),
                  pl.BlockSpec((1, 1, dr), lambda b, pt: (b, 0, 0)),
                  pl.BlockSpec(memory_space=pl.ANY),
                  pl.BlockSpec(memory_space=pl.ANY)],
        out_specs=pl.BlockSpec((1, h, r), lambda b, pt: (b, 0, 0)),
        scratch_shapes=[pltpu.VMEM((2, pg, PAGE_SIZE, r), F32),
                        pltpu.VMEM((2, pg, dr, PAGE_SIZE), F32),
                        pltpu.SemaphoreType.DMA((2, 2))],
    )
    return pl.pallas_call(
        functools.partial(_decode_kernel, n_pages=n_pages, pg=pg, sub_pg=min(DECODE_SUB_PAGES, pg), n_seq=db),
        grid_spec=grid_spec,
        out_shape=jax.ShapeDtypeStruct((db, h, r), F32),
        compiler_params=_cparams(("arbitrary",)),
        name="mla_decode",
    )(page_table.reshape(-1), qx, ckn, kpn, cache_c, cache_pt)


def _mlaout_kernel(x_ref, ol_ref, wuv_ref, wo_ref, gt_ref, gpost_ref, o_ref, *, cdt):
    t = jnp.concatenate([_dot(ol_ref[h], wuv_ref[h], cdt).astype(cdt) for h in range(MLA_HEADS)], axis=-1)
    y = _dot(t, wo_ref[...], cdt)
    o_ref[...] = x_ref[...] + gt_ref[0] * _rms(y, gpost_ref[...])


def _mlaout(x, ol, wuv, wo, gate, gpost, *, per_row, seq_len, cdt, tm_pref):
    n, d = x.shape
    tm = _tile(seq_len if not per_row else n, tm_pref)
    tps = max(seq_len // tm, 1)
    mspec = _mod_spec(per_row, tm, tps, d)
    return pl.pallas_call(
        functools.partial(_mlaout_kernel, cdt=cdt),
        grid=(n // tm,),
        in_specs=[pl.BlockSpec((tm, d), lambda i: (i, 0)),
                  pl.BlockSpec((MLA_HEADS, tm, MLA_KV_RANK), lambda i: (0, i, 0)),
                  pl.BlockSpec(wuv.shape, lambda i: (0, 0, 0)),
                  pl.BlockSpec(wo.shape, lambda i: (0, 0)),
                  mspec, pl.BlockSpec((1, d), lambda i: (0, 0))],
        out_specs=pl.BlockSpec((tm, d), lambda i: (i, 0)),
        out_shape=jax.ShapeDtypeStruct((n, d), F32),
        compiler_params=_cparams(("parallel",)),
        name="mla_out",
    )(x, ol, wuv, wo, gate, gpost)


def _prep_weights(p, cdt):
    d = p['w_in_ab'].shape[0]
    n_main = 4 * HG_HEADS * HG_KEY + SSM_INNER + 2 * SSM_INNER
    w_in = p['w_in_ab']
    w_dt = jnp.pad(w_in[:, n_main:], ((0, 0), (0, LANES - SSM_HEADS)))
    exch = lambda a: jnp.concatenate([a, jnp.roll(a[..., -MLA_ROPE:], MLA_ROPE // 2, axis=-1)], axis=-1)
    w_dkv = exch(p['mla_w_dkv'])
    w_uq = exch(p['mla_w_uq'].reshape(MLA_Q_RANK, MLA_HEADS, MLA_NOPE + MLA_ROPE)).transpose(1, 0, 2)
    c = lambda a: a.astype(cdt)
    return dict(
        ffn_wg=c(p['ffn_wg']), ffn_wu=c(p['ffn_wu']), ffn_wd=c(p['ffn_wd']),
        w_in=c(w_in[:, :n_main]), w_dt=c(w_dt), w_out=c(p['w_out_ab']),
        w_dq=c(p['mla_w_dq']), w_dkv=c(w_dkv),
        w_uq=c(w_uq),
        w_uk=c(p['mla_w_uk'].transpose(1, 2, 0)),
        w_uv=c(p['mla_w_uv'].transpose(1, 0, 2)),
        w_o=c(p['mla_w_o']),
    )


def _rope_tables(pos):
    half = MLA_ROPE // 2
    inv = ROPE_THETA ** (-jnp.arange(half, dtype=F32) / half)
    ang = pos[:, None] * inv[None]
    cos, sin = jnp.cos(ang), jnp.sin(ang)
    return jnp.concatenate([cos, cos, -sin, sin], axis=-1)


def _trunk(x, mod_all, pos, hg_s0, ssm_s0, conv_prev, past, p, w, consts, *, bsz, seq_len, cdt, cfg):
    n, d = x.shape
    per_row = seq_len == 1
    vec = lambda a: a.reshape(1, -1)

    def mods(layer, sub):
        m = mod_all[layer].reshape(bsz, 3, 3, d)[:, sub]
        if per_row:
            return tuple(m[:, k].reshape(1, bsz, d) for k in range(3))
        return tuple(m[:, k].reshape(bsz, 1, d) for k in range(3))

    common = dict(per_row=per_row, seq_len=seq_len, cdt=cdt)

    def ffn(x, layer, which, sub):
        return _ffn(x, mods(layer, sub), vec(p['norm_pre'][layer, sub]), vec(p['norm_post'][layer, sub]),
                    w['ffn_wg'][layer, which], w['ffn_wu'][layer, which], w['ffn_wd'][layer, which],
                    tm_pref=cfg['tm_ffn'], fb=cfg['fb'], **common)

    x = ffn(x, 0, 0, 0)
    m1 = mods(0, 1)
    proj, dtraw = _modmm(x, m1, vec(p['norm_pre'][0, 1]), w['w_in'], w['w_dt'],
                         tm_pref=cfg['tm_in'], tn=SSM_INNER, **common)
    lb = jnp.cumsum(jax.nn.softmax(p['hg_lb_logits'].astype(F32), axis=0), axis=0)[0]
    pad16 = lambda a: jnp.pad(a.astype(F32), (0, LANES - SSM_HEADS)).reshape(1, LANES)
    ssm_vecs = (p['ssm_conv_w'], vec(p['ssm_conv_b']), pad16(p['ssm_dt_bias']), pad16(p['ssm_a_log']),
                vec(jnp.repeat(p['ssm_d'].astype(F32), SSM_HEAD_DIM)), vec(p['ssm_norm']))
    ssm_s0t = ssm_s0.transpose(0, 1, 3, 2)
    if per_row:
        nr = _tile(bsz, SUBLANES)
        o_a, hg_s = _hgrn_step(proj, vec(lb), vec(p['hg_norm']), hg_s0, nr=nr)
        y_b, ssm_st = _ssd_step(proj, dtraw, *ssm_vecs, consts['emat'], conv_prev, ssm_s0t, nr=nr)
    else:
        o_a, hg_s = _hgrn(proj, vec(lb), vec(p['hg_norm']), hg_s0, bsz=bsz, l_pad=seq_len, l_true=seq_len,
                          chunk=cfg['hg_chunk'], n_inner=cfg['hg_inner'], hps=cfg['hg_hps'], cdt=cdt)
        cprev8 = jnp.pad(conv_prev, ((0, 0), (SUBLANES - (SSM_CONV - 1), 0), (0, 0)))
        y_b, ssm_st = _ssd(proj, dtraw, *ssm_vecs, consts['emat'], cprev8, ssm_s0t,
                           bsz=bsz, l_pad=seq_len, l_true=seq_len, chunk=cfg['ssd_chunk'], cdt=cdt)
    ssm_s = ssm_st.transpose(0, 1, 3, 2)
    n_tail = min(seq_len, SSM_CONV - 1)
    xbc_tail = proj.reshape(bsz, seq_len, -1)[:, seq_len - n_tail:, 5 * SSM_INNER:7 * SSM_INNER]
    conv_s = jnp.concatenate([conv_prev, xbc_tail], axis=1)[:, -(SSM_CONV - 1):]
    x = _outproj(x, o_a, y_b, w['w_out'], m1[2], vec(p['norm_post'][0, 1]), tm_pref=cfg['tm_mm'], **common)
    x = ffn(x, 0, 1, 2)

    x = ffn(x, 1, 0, 0)
    m1 = mods(1, 1)
    qraw, kvraw = _modmm(x, m1, vec(p['norm_pre'][1, 1]), w['w_dq'], w['w_dkv'],
                         tm_pref=cfg['tm_mm'], tn=MLA_Q_RANK, **common)
    q_scale = 1.0 if past is not None else MLA_SCALE * math.log2(math.e)
    qx, ckv, kpe, kx, kxt = _mlaq(qraw, kvraw, vec(p['mla_g_q']), vec(p['mla_g_kv']), w['w_uq'], w['w_uk'],
                                  pos, cdt=cdt, tm_pref=cfg['tm_mm'], q_scale=q_scale)
    if past is None:
        ol = _attn(qx, kx, kxt, bsz=bsz, seq_len=seq_len, tq=cfg['tq'], tk=cfg['tk'], hc=cfg['hc'])
    else:
        cache_c, cache_p, page_table = past
        o_dec = _decode(page_table, qx.transpose(1, 0, 2), ckv.reshape(bsz, 1, -1), kpe.reshape(bsz, 1, -1),
                        cache_c, cache_p.transpose(0, 2, 1), pg=cfg['pg'])
        ol = o_dec.transpose(1, 0, 2)
    x = _mlaout(x, ol, w['w_uv'], w['w_o'], m1[2], vec(p['norm_post'][1, 1]), tm_pref=cfg['tm_mm'], **common)
    x = ffn(x, 1, 1, 2)
    return x, hg_s, ssm_s, conv_s, ckv, kpe


def kernel(x_prompt, x_sample, c_prompt, c_sample, state_hgrn, state_ssm, state_conv, cache_ckv, cache_kpe, page_table, ada_w, ada_b, norm_pre, norm_post, ffn_wg, ffn_wu, ffn_wd, w_in_ab, w_out_ab, hg_lb_logits, hg_norm, ssm_conv_w, ssm_conv_b, ssm_dt_bias, ssm_a_log, ssm_d, ssm_norm, mla_w_dq, mla_g_q, mla_w_uq, mla_w_dkv, mla_g_kv, mla_w_uk, mla_w_uv, mla_w_o):
    p = dict(norm_pre=norm_pre, norm_post=norm_post, ffn_wg=ffn_wg, ffn_wu=ffn_wu, ffn_wd=ffn_wd,
             w_in_ab=w_in_ab, w_out_ab=w_out_ab, hg_lb_logits=hg_lb_logits, hg_norm=hg_norm,
             ssm_conv_w=ssm_conv_w, ssm_conv_b=ssm_conv_b, ssm_dt_bias=ssm_dt_bias, ssm_a_log=ssm_a_log,
             ssm_d=ssm_d, ssm_norm=ssm_norm, mla_w_dq=mla_w_dq, mla_g_q=mla_g_q, mla_w_uq=mla_w_uq,
             mla_w_dkv=mla_w_dkv, mla_g_kv=mla_g_kv, mla_w_uk=mla_w_uk, mla_w_uv=mla_w_uv, mla_w_o=mla_w_o)
    bp, seq, d = x_prompt.shape
    db, dseq, _ = x_sample.shape
    assert dseq == 1
    n_pages = page_table.shape[1]
    past_len = n_pages * PAGE_SIZE
    f = ffn_wg.shape[-1]

    head_of = np.arange(SSM_INNER) // SSM_HEAD_DIM
    emat = jnp.asarray((np.arange(LANES)[:, None] == head_of[None, :]).astype(np.float32), dtype=BF16)
    consts = dict(emat=emat)

    mod_all = _ada_mod(jnp.concatenate([c_prompt, c_sample], axis=0), ada_w, ada_b)

    fb = 2 * LANES if f % (2 * LANES) == 0 else f
    w_lo = _prep_weights(p, BF16)
    cfg_p = dict(tm_ffn=512, fb=fb, tm_mm=512, tm_in=256, l_pad=seq, hg_chunk=min(64, seq), hg_inner=max(1, min(16, seq // 64)),
                 hg_hps=1,
                 ssd_chunk=min(256, seq), tq=min(128, seq), tk=min(512, seq), hc=2)
    hg0 = jnp.zeros((bp, HG_HEADS, HG_KEY, HG_VAL), F32)
    ssm0 = jnp.zeros((bp, SSM_HEADS, SSM_STATE, SSM_HEAD_DIM), F32)
    conv0 = jnp.zeros((bp, SSM_CONV - 1, 2 * SSM_INNER), F32)
    pos_p = jnp.tile(_rope_tables(jnp.arange(seq, dtype=F32)), (bp, 1))
    y_p, hg_p, ssm_p, conv_p, ckv_p, kpe_p = _trunk(
        x_prompt.reshape(bp * seq, d), mod_all[:, :bp], pos_p, hg0, ssm0, conv0, None, p, w_lo, consts,
        bsz=bp, seq_len=seq, cdt=BF16, cfg=cfg_p)

    w_hi = _prep_weights(p, F32)
    pg = 64 if n_pages % 64 == 0 else n_pages
    cfg_s = dict(tm_ffn=128, fb=fb, tm_mm=128, tm_in=128, l_pad=SUBLANES, hg_chunk=SUBLANES, hg_inner=1, hg_hps=HG_HEADS,
                 ssd_chunk=SUBLANES, pg=pg)
    pos_s = _rope_tables(jnp.full((db,), past_len, F32))
    y_s, hg_s, ssm_s, conv_s, ckv_s, kpe_s = _trunk(
        x_sample.reshape(db, d), mod_all[:, bp:], pos_s, state_hgrn, state_ssm, state_conv,
        (cache_ckv, cache_kpe, page_table), p, w_hi, consts, bsz=db, seq_len=1, cdt=F32, cfg=cfg_s)

    return (y_p.reshape(bp, seq, d), y_s.reshape(db, 1, d), hg_p, hg_s, ssm_p, ssm_s, conv_p, conv_s,
            ckv_p.reshape(bp, seq, -1), ckv_s.reshape(db, 1, -1), kpe_p.reshape(bp, seq, -1),
            kpe_s.reshape(db, 1, -1))
```
